```python
import jax
import jax.numpy as jnp
from jax import lax
import numpy as np

D_MODEL = 1024
BATCH = 8
SEQ = 2048
DEPTH = 4
DEC_BATCH = 128
DEC_SEQ = 8
PAST_LEN = 16384
PAGE_SIZE = 128

HEAD_SIZE = 64
N_HEADS = D_MODEL // HEAD_SIZE
DECAY_LORA = 64
AAA_LORA = 64
MV_LORA = 32
GATE_LORA = 160
CONV_WIDTH = 31
FFN_CONV_WIDTH = 3
D_FF = ((8 * D_MODEL // 3 + 127) // 128) * 128
N_RWKV = (DEPTH + 1) // 2
N_CONV = DEPTH // 2
NORM_EPS = 1e-6
LN_EPS = 1e-5
LNX_EPS = 64e-5

kernel_name = 'rwkv7_conformer_convffn_step'


def _rmsnorm(x, g):
    xf = x.astype(jnp.float32)
    y = xf * lax.rsqrt(jnp.mean(xf * xf, axis=-1, keepdims=True) + NORM_EPS)
    return (y * g.astype(jnp.float32)).astype(x.dtype)


def _layernorm(x, g, b, eps):
    xf = x.astype(jnp.float32)
    mu = jnp.mean(xf, axis=-1, keepdims=True)
    var = jnp.mean(jnp.square(xf - mu), axis=-1, keepdims=True)
    y = (xf - mu) * lax.rsqrt(var + eps) * g.astype(jnp.float32) + b.astype(jnp.float32)
    return y.astype(x.dtype)


def _causal_dwconv(x, buf, w, b):
    xp = jnp.concatenate([buf.astype(x.dtype), x], axis=1)
    y = lax.conv_general_dilated(xp, w[:, None, :].astype(x.dtype), window_strides=(1,),
                                 padding='VALID', dimension_numbers=('NWC', 'WIO', 'NWC'),
                                 feature_group_count=x.shape[-1])
    return y + b.astype(x.dtype), xp[:, xp.shape[1] - (w.shape[0] - 1):]


def _wkv7_scan(s0, r, w, k, v, a, b):
    def step(s, inp):
        r_t, w_t, k_t, v_t, a_t, b_t = inp
        sa = jnp.einsum('bhij,bhj->bhi', s, a_t)
        s = (s * w_t[:, :, None, :] + sa[..., None] * b_t[:, :, None, :]
             + v_t[..., None] * k_t[:, :, None, :])
        return s, jnp.einsum('bhij,bhj->bhi', s, r_t)
    xs = tuple(jnp.swapaxes(t.astype(jnp.float32), 0, 1) for t in (r, w, k, v, a, b))
    s_final, ys = lax.scan(step, s0.astype(jnp.float32), xs)
    return jnp.swapaxes(ys, 0, 1), s_final


def _rwkv7_time_mix(x, shift_prev, s0, v_first, vres, mix, wr, wk, wv, wo, w0, w1, w2,
                    a0, a1, a2, g1, g2, k_k, k_a, r_k, lnx_w, lnx_b):
    B, T, C = x.shape
    xprev = jnp.concatenate([shift_prev[:, None, :].astype(x.dtype), x[:, :-1]], axis=1)
    xx = xprev - x
    xr, xw, xk, xv, xa, xg = (x + xx * mix[m] for m in range(6))
    r = xr @ wr
    w_log = -jax.nn.softplus(-(w0 + jnp.tanh(xw @ w1) @ w2)) - 0.5
    k = xk @ wk
    v = xv @ wv
    if v_first is None:
        v_first = v
    else:
        v0, v1, v2 = vres
        v = v + (v_first - v) * jax.nn.sigmoid(v0 + (xv @ v1) @ v2)
    a = jax.nn.sigmoid(a0 + (xa @ a1) @ a2)
    g = jax.nn.sigmoid(xg @ g1) @ g2

    def heads(t):
        return t.reshape(B, T, N_HEADS, HEAD_SIZE).astype(jnp.float32)

    kk = heads(k * k_k)
    kk = kk / jnp.maximum(jnp.sqrt(jnp.sum(kk * kk, axis=-1, keepdims=True)), 1e-12)
    k = k * (1 + (a - 1) * k_a)
    decay = jnp.exp(-jnp.exp(heads(w_log)))
    rh, kh, vh = heads(r), heads(k), heads(v)
    y, s_new = _wkv7_scan(s0, rh, decay, kh, vh, -kk, kk * heads(a))
    mu = jnp.mean(y, axis=-1, keepdims=True)
    var = jnp.mean(jnp.square(y - mu), axis=-1, keepdims=True)
    y = ((y - mu) * lax.rsqrt(var + LNX_EPS)).reshape(B, T, C)
    y = y * lnx_w.astype(jnp.float32) + lnx_b.astype(jnp.float32)
    bonus = (jnp.sum(rh * kh * r_k.astype(jnp.float32), axis=-1, keepdims=True) * vh).reshape(B, T, C)
    out = ((y + bonus).astype(x.dtype) * g) @ wo
    return out, x[:, -1], s_new, v_first


def _conformer_conv(x, buf, w_in, b_in, dw_w, dw_b, ln_w, ln_b, w_out, b_out):
    C = x.shape[-1]
    h = x @ w_in + b_in
    h = h[..., :C] * jax.nn.sigmoid(h[..., C:])
    h, new_buf = _causal_dwconv(h, buf, dw_w, dw_b)
    h = jax.nn.silu(_layernorm(h, ln_w, ln_b, LN_EPS))
    return h @ w_out + b_out, new_buf


def _conv_ffn(x, buf, w_in, dw_w, dw_b, w_out):
    h, new_buf = _causal_dwconv(x @ w_in, buf, dw_w, dw_b)
    g, u = jnp.split(h, 2, axis=-1)
    return (jax.nn.silu(g) * u) @ w_out, new_buf


def _trunk(x, st_wkv, st_shift, st_glu, st_ffn, p):
    v_first = None
    wkv_l, shift_l, glu_l, ffn_l = [], [], [], []
    for i in range(DEPTH):
        j = i // 2
        h = _rmsnorm(x, p['norm_gain'][i, 0])
        if i % 2 == 0:
            vres = None if j == 0 else (p['rwkv_v0'][j - 1], p['rwkv_v1'][j - 1], p['rwkv_v2'][j - 1])
            m, sh, s_new, v_first = _rwkv7_time_mix(
                h, st_shift[j], st_wkv[j], v_first, vres, p['rwkv_mix'][j],
                p['rwkv_wr'][j], p['rwkv_wk'][j], p['rwkv_wv'][j], p['rwkv_wo'][j],
                p['rwkv_w0'][j], p['rwkv_w1'][j], p['rwkv_w2'][j],
                p['rwkv_a0'][j], p['rwkv_a1'][j], p['rwkv_a2'][j],
                p['rwkv_g1'][j], p['rwkv_g2'][j], p['rwkv_k_k'][j], p['rwkv_k_a'][j],
                p['rwkv_r_k'][j], p['rwkv_lnx_w'][j], p['rwkv_lnx_b'][j])
            wkv_l.append(s_new)
            shift_l.append(sh)
        else:
            m, gb = _conformer_conv(
                h, st_glu[j], p['conv_w_in'][j], p['conv_b_in'][j], p['conv_dw_w'][j],
                p['conv_dw_b'][j], p['conv_ln_w'][j], p['conv_ln_b'][j],
                p['conv_w_out'][j], p['conv_b_out'][j])
            glu_l.append(gb)
        x = x + _rmsnorm(m, p['norm_gain'][i, 1])
        h = _rmsnorm(x, p['norm_gain'][i, 2])
        f, fb = _conv_ffn(h, st_ffn[i], p['ffn_w_in'][i], p['ffn_dw_w'][i],
                          p['ffn_dw_b'][i], p['ffn_w_out'][i])
        ffn_l.append(fb)
        x = x + _rmsnorm(f, p['norm_gain'][i, 3])
    return x, jnp.stack(wkv_l), jnp.stack(shift_l), jnp.stack(glu_l), jnp.stack(ffn_l)


def setup_inputs(seed: int = 0) -> dict:
    key = jax.random.key(seed)
    ks = iter(jax.random.split(key, 64))
    D = D_MODEL
    F2 = 2 * D_FF

    def nrm(shape, scale):
        return scale * jax.random.normal(next(ks), shape, jnp.float32)

    def uni(shape, lo, hi):
        return jax.random.uniform(next(ks), shape, jnp.float32, lo, hi)

    n_vres = N_RWKV - 1
    return {
        'x_prompt': nrm((BATCH, SEQ, D), 1.0),
        'x_sample': nrm((DEC_BATCH, DEC_SEQ, D), 1.0),
        'state_rwkv_wkv': nrm((N_RWKV, DEC_BATCH, N_HEADS, HEAD_SIZE, HEAD_SIZE), 0.1),
        'state_rwkv_shift': nrm((N_RWKV, DEC_BATCH, D), 1.0),
        'state_conv_glu': nrm((N_CONV, DEC_BATCH, CONV_WIDTH - 1, D), 0.5),
        'state_ffn_conv': nrm((DEPTH, DEC_BATCH, FFN_CONV_WIDTH - 1, F2), 0.5),
        'norm_gain': 1.0 + nrm((DEPTH, 4, D), 0.05),
        'rwkv_mix': uni((N_RWKV, 6, D), 0.0, 1.0),
        'rwkv_wr': nrm((N_RWKV, D, D), D ** -0.5),
        'rwkv_wk': nrm((N_RWKV, D, D), D ** -0.5),
        'rwkv_wv': nrm((N_RWKV, D, D), D ** -0.5),
        'rwkv_wo': nrm((N_RWKV, D, D), D ** -0.5),
        'rwkv_w0': uni((N_RWKV, D), -6.0, -0.5),
        'rwkv_w1': nrm((N_RWKV, D, DECAY_LORA), D ** -0.5),
        'rwkv_w2': nrm((N_RWKV, DECAY_LORA, D), 0.3 * DECAY_LORA ** -0.5),
        'rwkv_a0': nrm((N_RWKV, D), 0.3),
        'rwkv_a1': nrm((N_RWKV, D, AAA_LORA), D ** -0.5),
        'rwkv_a2': nrm((N_RWKV, AAA_LORA, D), 0.3 * AAA_LORA ** -0.5),
        'rwkv_v0': 1.0 + nrm((n_vres, D), 0.3),
        'rwkv_v1': nrm((n_vres, D, MV_LORA), D ** -0.5),
        'rwkv_v2': nrm((n_vres, MV_LORA, D), 0.3 * MV_LORA ** -0.5),
        'rwkv_g1': nrm((N_RWKV, D, GATE_LORA), D ** -0.5),
        'rwkv_g2': nrm((N_RWKV, GATE_LORA, D), GATE_LORA ** -0.5),
        'rwkv_k_k': 0.85 + nrm((N_RWKV, D), 0.05),
        'rwkv_k_a': 1.0 + nrm((N_RWKV, D), 0.05),
        'rwkv_r_k': nrm((N_RWKV, N_HEADS, HEAD_SIZE), 0.1),
        'rwkv_lnx_w': 1.0 + nrm((N_RWKV, D), 0.05),
        'rwkv_lnx_b': nrm((N_RWKV, D), 0.01),
        'conv_w_in': nrm((N_CONV, D, 2 * D), D ** -0.5),
        'conv_b_in': nrm((N_CONV, 2 * D), 0.01),
        'conv_dw_w': nrm((N_CONV, CONV_WIDTH, D), CONV_WIDTH ** -0.5),
        'conv_dw_b': nrm((N_CONV, D), 0.01),
        'conv_ln_w': 1.0 + nrm((N_CONV, D), 0.05),
        'conv_ln_b': nrm((N_CONV, D), 0.01),
        'conv_w_out': nrm((N_CONV, D, D), D ** -0.5),
        'conv_b_out': nrm((N_CONV, D), 0.01),
        'ffn_w_in': nrm((DEPTH, D, F2), D ** -0.5),
        'ffn_dw_w': nrm((DEPTH, FFN_CONV_WIDTH, F2), FFN_CONV_WIDTH ** -0.5),
        'ffn_dw_b': nrm((DEPTH, F2), 0.01),
        'ffn_w_out': nrm((DEPTH, D_FF, D), D_FF ** -0.5),
    }


def reference(x_prompt, x_sample, state_rwkv_wkv, state_rwkv_shift, state_conv_glu,
              state_ffn_conv, norm_gain, rwkv_mix, rwkv_wr, rwkv_wk, rwkv_wv, rwkv_wo,
              rwkv_w0, rwkv_w1, rwkv_w2, rwkv_a0, rwkv_a1, rwkv_a2, rwkv_v0, rwkv_v1,
              rwkv_v2, rwkv_g1, rwkv_g2, rwkv_k_k, rwkv_k_a, rwkv_r_k, rwkv_lnx_w,
              rwkv_lnx_b, conv_w_in, conv_b_in, conv_dw_w, conv_dw_b, conv_ln_w,
              conv_ln_b, conv_w_out, conv_b_out, ffn_w_in, ffn_dw_w, ffn_dw_b, ffn_w_out):
    p = dict(norm_gain=norm_gain, rwkv_mix=rwkv_mix, rwkv_wr=rwkv_wr, rwkv_wk=rwkv_wk,
             rwkv_wv=rwkv_wv, rwkv_wo=rwkv_wo, rwkv_w0=rwkv_w0, rwkv_w1=rwkv_w1,
             rwkv_w2=rwkv_w2, rwkv_a0=rwkv_a0, rwkv_a1=rwkv_a1, rwkv_a2=rwkv_a2,
             rwkv_v0=rwkv_v0, rwkv_v1=rwkv_v1, rwkv_v2=rwkv_v2, rwkv_g1=rwkv_g1,
             rwkv_g2=rwkv_g2, rwkv_k_k=rwkv_k_k, rwkv_k_a=rwkv_k_a, rwkv_r_k=rwkv_r_k,
             rwkv_lnx_w=rwkv_lnx_w, rwkv_lnx_b=rwkv_lnx_b, conv_w_in=conv_w_in,
             conv_b_in=conv_b_in, conv_dw_w=conv_dw_w, conv_dw_b=conv_dw_b,
             conv_ln_w=conv_ln_w, conv_ln_b=conv_ln_b, conv_w_out=conv_w_out,
             conv_b_out=conv_b_out, ffn_w_in=ffn_w_in, ffn_dw_w=ffn_dw_w,
             ffn_dw_b=ffn_dw_b, ffn_w_out=ffn_w_out)
    bp = x_prompt.shape[0]
    dt = x_prompt.dtype
    z_wkv = jnp.zeros((N_RWKV, bp, N_HEADS, HEAD_SIZE, HEAD_SIZE), jnp.float32)
    z_shift = jnp.zeros((N_RWKV, bp, D_MODEL), dt)
    z_glu = jnp.zeros((N_CONV, bp, CONV_WIDTH - 1, D_MODEL), dt)
    z_ffn = jnp.zeros((DEPTH, bp, FFN_CONV_WIDTH - 1, 2 * D_FF), dt)
    y_prompt, p_wkv, p_shift, p_glu, p_ffn = _trunk(x_prompt, z_wkv, z_shift, z_glu, z_ffn, p)
    y_sample, s_wkv, s_shift, s_glu, s_ffn = _trunk(
        x_sample, state_rwkv_wkv, state_rwkv_shift, state_conv_glu, state_ffn_conv, p)
    p_wkv = p_wkv.astype(state_rwkv_wkv.dtype)
    s_wkv = s_wkv.astype(state_rwkv_wkv.dtype)
    return (y_prompt, y_sample, p_wkv, p_shift, p_glu, p_ffn, s_wkv, s_shift, s_glu, s_ffn)
```

```python
import functools

import jax
import jax.numpy as jnp
from jax import lax
from jax.experimental import pallas as pl
from jax.experimental.pallas import tpu as pltpu

D_MODEL = 1024
HEAD_SIZE = 64
N_HEADS = D_MODEL // HEAD_SIZE
DEPTH = 4
CONV_WIDTH = 31
FFN_CONV_WIDTH = 3
NORM_EPS = 1e-6
LN_EPS = 1e-5
LNX_EPS = 64e-5

LANES = 128
SUBLANES = 8
ROWS = 256
FFN_CHUNK = 256
WKV_T_CHUNK = 32
VMEM_LIMIT = 56 * 1024 * 1024

_BF = jnp.bfloat16
_F32 = jnp.float32


def _tiles(B, T, max_bb=None):
    Tt = min(T, ROWS // SUBLANES)
    Bb = min(B, ROWS // Tt)
    if max_bb is not None:
        Bb = min(Bb, max_bb)
    assert T % Tt == 0 and B % Bb == 0 and Tt % SUBLANES == 0, (B, T)
    return Bb, Tt


def _wspec(shape):
    nd = len(shape)
    return pl.BlockSpec(shape, lambda *_: (0,) * nd, pipeline_mode=pl.Buffered(1))


def _rms(x, g):
    return x * lax.rsqrt(jnp.mean(x * x, axis=-1, keepdims=True) + NORM_EPS) * g


def _bdot(a, w):
    return jnp.dot(a.astype(_BF), w, preferred_element_type=_F32)


def _sigmoid(x):
    return 1.0 / (1.0 + jnp.exp(-x))


def _params(n_par):
    return pltpu.CompilerParams(
        dimension_semantics=("parallel",) * n_par + ("arbitrary",),
        vmem_limit_bytes=VMEM_LIMIT)


def _rwkv_proj_kernel(has_vres, Bb, Tt, *refs):
    refs = list(refs)
    x_ref, shift_ref = refs[:2]
    refs = refs[2:]
    if has_vres:
        vfirst_ref = refs.pop(0)
    (gain_ref, mix_ref, wr_ref, wk_ref, wv_ref, w0_ref, w1_ref, w2_ref,
     a0_ref, a1_ref, a2_ref, g1_ref, g2_ref) = refs[:13]
    refs = refs[13:]
    if has_vres:
        v0_ref, v1_ref, v2_ref = refs[:3]
        refs = refs[3:]
    r_out, w_out, k_out, v_out, a_out, g_out, shift_out, xs = refs

    R = Bb * Tt
    t = pl.program_id(1)
    h = _rms(x_ref[...].reshape(R, D_MODEL), gain_ref[...])
    h3 = h.reshape(Bb, Tt, D_MODEL)

    @pl.when(t == 0)
    def _():
        xs[:, 7:8, :] = shift_ref[...]

    xs[:, 8:8 + Tt, :] = h3
    xx = (xs[:, 7:7 + Tt, :] - h3).reshape(R, D_MODEL)
    last = h3[:, Tt - 1:Tt, :]
    xs[:, 7:8, :] = last
    shift_out[...] = last

    def mixed(m):
        return h + xx * mix_ref[m:m + 1, :]

    def out3(v):
        return v.reshape(Bb, Tt, D_MODEL)

    r_out[...] = out3(_bdot(mixed(0), wr_ref[...]))
    wl = w0_ref[...] + _bdot(jnp.tanh(_bdot(mixed(1), w1_ref[...])), w2_ref[...])
    w_log = -(jnp.maximum(-wl, 0.0) + jnp.log(1.0 + jnp.exp(-jnp.abs(wl)))) - 0.5
    w_out[...] = out3(jnp.exp(-jnp.exp(w_log)))
    k_out[...] = out3(_bdot(mixed(2), wk_ref[...]))
    xv = mixed(3)
    v = _bdot(xv, wv_ref[...])
    if has_vres:
        vf = vfirst_ref[...].reshape(R, D_MODEL)
        v = v + (vf - v) * _sigmoid(v0_ref[...] + _bdot(_bdot(xv, v1_ref[...]), v2_ref[...]))
    v_out[...] = out3(v)
    a_out[...] = out3(_sigmoid(a0_ref[...] + _bdot(_bdot(mixed(4), a1_ref[...]), a2_ref[...])))
    g_out[...] = out3(_bdot(_sigmoid(_bdot(mixed(5), g1_ref[...])), g2_ref[...]))


def _rwkv_proj(x, shift, v_first, gain, lw):
    B, T, D = x.shape
    Bb, Tt = _tiles(B, T)
    has_vres = v_first is not None
    row = pl.BlockSpec((Bb, Tt, D), lambda b, t: (b, t, 0))
    st = pl.BlockSpec((Bb, 1, D), lambda b, t: (b, 0, 0))
    args = [x, shift.reshape(B, 1, D)]
    specs = [row, st]
    if has_vres:
        args.append(v_first)
        specs.append(row)
    names = ['mix', 'wr', 'wk', 'wv', 'w0', 'w1', 'w2', 'a0', 'a1', 'a2', 'g1', 'g2']
    if has_vres:
        names += ['v0', 'v1', 'v2']
    wargs = [gain] + [lw[n] for n in names]
    args += wargs
    specs += [_wspec(w.shape) for w in wargs]
    act = jax.ShapeDtypeStruct((B, T, D), _F32)
    outs = pl.pallas_call(
        functools.partial(_rwkv_proj_kernel, has_vres, Bb, Tt),
        grid=(B // Bb, T // Tt),
        in_specs=specs,
        out_specs=[row] * 6 + [st],
        out_shape=[act] * 6 + [jax.ShapeDtypeStruct((B, 1, D), _F32)],
        scratch_shapes=[pltpu.VMEM((Bb, SUBLANES + Tt, D), _F32)],
        compiler_params=_params(1),
        name='rwkv_proj_vres' if has_vres else 'rwkv_proj',
    )(*args)
    r, w, k, v, a, g, sh = outs
    return r, w, k, v, a, g, sh.reshape(B, D)


def _wkv_kernel(Tc, r_ref, w_ref, k_ref, v_ref, a_ref, kk_ref, ka_ref, rk_ref, lnw_ref,
                lnb_ref, s0_ref, o_ref, s_ref, an_s, bn_s, km_s):
    N = HEAD_SIZE

    @pl.when(pl.program_id(1) == 0)
    def _():
        s_ref[...] = s0_ref[...]

    def row(ref, j):
        return ref[pl.ds(j, 1), :]

    def step(t, carry):
        k = k_ref[t]
        a = a_ref[t]
        v = v_ref[t]
        kk = k * kk_ref[...]
        kk = kk / jnp.maximum(jnp.sqrt(jnp.sum(kk * kk, axis=0, keepdims=True)), 1e-12)
        km = k * (1.0 + (a - 1.0) * ka_ref[...])
        an_s[...] = -kk
        bn_s[...] = kk * a
        km_s[...] = km

        sa = jnp.zeros((N, LANES), _F32)
        for j in range(N):
            sa = sa + s_ref[j] * row(an_s, j)

        y = jnp.zeros((N, LANES), _F32)
        for j in range(N):
            sn = (s_ref[j] * w_ref[t, pl.ds(j, 1), :] + sa * row(bn_s, j) + v * row(km_s, j))
            s_ref[j] = sn
            y = y + sn * r_ref[t, pl.ds(j, 1), :]

        mu = jnp.mean(y, axis=0, keepdims=True)
        d = y - mu
        var = jnp.mean(d * d, axis=0, keepdims=True)
        yn = d * lax.rsqrt(var + LNX_EPS) * lnw_ref[...] + lnb_ref[...]
        bonus = jnp.sum(r_ref[t] * km * rk_ref[...], axis=0, keepdims=True) * v
        o_ref[t] = yn + bonus
        return carry

    lax.fori_loop(0, Tc, step, 0)


def _to_scan(x):
    B, T, _ = x.shape
    return x.reshape(B, T, N_HEADS, HEAD_SIZE).transpose(1, 3, 0, 2).reshape(T, HEAD_SIZE, B * N_HEADS)


def _from_scan(o, B):
    T = o.shape[0]
    return o.reshape(T, HEAD_SIZE, B, N_HEADS).transpose(2, 0, 3, 1).reshape(B, T, D_MODEL)


def _lane_param(p):
    return jnp.tile(p.reshape(N_HEADS, HEAD_SIZE).T, (1, LANES // N_HEADS))


def _wkv(r, w, k, v, a, s0, lw):
    B, T, _ = r.shape
    L = B * N_HEADS
    assert L % LANES == 0
    Tc = min(T, WKV_T_CHUNK)
    assert T % Tc == 0
    N = HEAD_SIZE
    seq = [_to_scan(z) for z in (r, w, k, v, a)]
    s0t = s0.astype(_F32).transpose(3, 2, 0, 1).reshape(N, N, L)
    pars = [_lane_param(lw[n]) for n in ('k_k', 'k_a', 'r_k', 'lnx_w', 'lnx_b')]
    seq_spec = pl.BlockSpec((Tc, N, LANES), lambda l, t: (t, 0, l))
    st_spec = pl.BlockSpec((N, N, LANES), lambda l, t: (0, 0, l))
    par_spec = pl.BlockSpec((N, LANES), lambda l, t: (0, 0))
    o, sT = pl.pallas_call(
        functools.partial(_wkv_kernel, Tc),
        grid=(L // LANES, T // Tc),
        in_specs=[seq_spec] * 5 + [par_spec] * 5 + [st_spec],
        out_specs=[seq_spec, st_spec],
        out_shape=[jax.ShapeDtypeStruct((T, N, L), _F32), jax.ShapeDtypeStruct((N, N, L), _F32)],
        scratch_shapes=[pltpu.VMEM((N, LANES), _F32)] * 3,
        compiler_params=_params(1),
        name='wkv_scan',
    )(*seq, *pars, s0t)
    s_new = sT.reshape(N, N, B, N_HEADS).transpose(2, 3, 1, 0)
    return _from_scan(o, B), s_new


def _rwkv_out_kernel(Bb, Tt, x_ref, o_ref, g_ref, wo_ref, gain_ref, y_ref):
    R = Bb * Tt
    m = _bdot((o_ref[...] * g_ref[...]).reshape(R, D_MODEL), wo_ref[...])
    y_ref[...] = x_ref[...] + _rms(m, gain_ref[...]).reshape(Bb, Tt, D_MODEL)


def _rwkv_out(x, o, g, wo, gain):
    B, T, D = x.shape
    Bb, Tt = _tiles(B, T)
    row = pl.BlockSpec((Bb, Tt, D), lambda b, t: (b, t, 0))
    return pl.pallas_call(
        functools.partial(_rwkv_out_kernel, Bb, Tt),
        grid=(B // Bb, T // Tt),
        in_specs=[row, row, row, _wspec(wo.shape), _wspec(gain.shape)],
        out_specs=row,
        out_shape=jax.ShapeDtypeStruct((B, T, D), _F32),
        compiler_params=_params(1),
        name='rwkv_out',
    )(x, o, g, wo, gain)


def _conformer_kernel(Bb, Tt, x_ref, st_ref, gin_ref, win_ref, bin_ref, dww_ref, dwb_ref,
                      lnw_ref, lnb_ref, wout_ref, bout_ref, gout_ref,
                      y_ref, st_out, buf, cbuf):
    R = Bb * Tt
    D = D_MODEL
    W1 = CONV_WIDTH - 1
    PAD = 32
    t = pl.program_id(1)
    x = x_ref[...].reshape(R, D)
    hh = _bdot(_rms(x, gin_ref[...]), win_ref[...]) + bin_ref[...]
    glu = hh[:, :D] * _sigmoid(hh[:, D:])

    @pl.when(t == 0)
    def _():
        buf[:, PAD - W1:PAD, :] = st_ref[...]

    buf[:, PAD:PAD + Tt, :] = glu.reshape(Bb, Tt, D)
    for c in range(D // LANES):
        cs = slice(c * LANES, (c + 1) * LANES)
        acc = jnp.zeros((Bb, Tt, LANES), _F32) + dwb_ref[:, cs]
        for kk in range(CONV_WIDTH):
            off = PAD - W1 + kk
            acc = acc + buf[:, off:off + Tt, cs] * dww_ref[kk:kk + 1, cs]
        cbuf[:, :, cs] = acc
    new_hist = buf[:, Tt + PAD - W1:Tt + PAD, :]
    st_out[...] = new_hist
    buf[:, PAD - W1:PAD, :] = new_hist

    cv = cbuf[...].reshape(R, D)
    mu = jnp.mean(cv, axis=-1, keepdims=True)
    d = cv - mu
    var = jnp.mean(d * d, axis=-1, keepdims=True)
    ln = d * lax.rsqrt(var + LN_EPS) * lnw_ref[...] + lnb_ref[...]
    m = _bdot(ln * _sigmoid(ln), wout_ref[...]) + bout_ref[...]
    y_ref[...] = (x + _rms(m, gout_ref[...])).reshape(Bb, Tt, D)


def _conformer(x, st, gin, gout, cw):
    B, T, D = x.shape
    Bb, Tt = _tiles(B, T)
    W1 = CONV_WIDTH - 1
    row = pl.BlockSpec((Bb, Tt, D), lambda b, t: (b, t, 0))
    stspec = pl.BlockSpec((Bb, W1, D), lambda b, t: (b, 0, 0))
    wargs = [gin, cw['w_in'], cw['b_in'], cw['dw_w'], cw['dw_b'], cw['ln_w'], cw['ln_b'],
             cw['w_out'], cw['b_out'], gout]
    y, st_new = pl.pallas_call(
        functools.partial(_conformer_kernel, Bb, Tt),
        grid=(B // Bb, T // Tt),
        in_specs=[row, stspec] + [_wspec(w.shape) for w in wargs],
        out_specs=[row, stspec],
        out_shape=[jax.ShapeDtypeStruct((B, T, D), _F32), jax.ShapeDtypeStruct((B, W1, D), _F32)],
        scratch_shapes=[pltpu.VMEM((Bb, 32 + Tt, D), _F32), pltpu.VMEM((Bb, Tt, D), _F32)],
        compiler_params=_params(1),
        name='conformer_conv',
    )(x, st, *wargs)
    return y, st_new


def _conv_ffn_kernel(Bb, Tt, F, x_ref, st_ref, gin_ref, win_ref, dww_ref, dwb_ref, wout_ref,
                     gout_ref, y_ref, st_out, buf):
    R = Bb * Tt
    D = D_MODEL
    W1 = FFN_CONV_WIDTH - 1
    PAD = SUBLANES
    t = pl.program_id(1)
    x = x_ref[...].reshape(R, D)
    up = _bdot(_rms(x, gin_ref[...]), win_ref[...])

    @pl.when(t == 0)
    def _():
        buf[:, PAD - W1:PAD, :] = st_ref[...]

    buf[:, PAD:PAD + Tt, :] = up.reshape(Bb, Tt, 2 * F)

    def conv(col0):
        cs = slice(col0, col0 + FFN_CHUNK)
        acc = dwb_ref[:, cs] + buf[:, PAD:PAD + Tt, cs] * dww_ref[W1:W1 + 1, cs]
        for kk in range(W1):
            off = PAD - W1 + kk
            acc = acc + buf[:, off:off + Tt, cs] * dww_ref[kk:kk + 1, cs]
        return acc.reshape(R, FFN_CHUNK)

    acc = jnp.zeros((R, D), _F32)
    for c in range(F // FFN_CHUNK):
        gc = conv(c * FFN_CHUNK)
        uc = conv(F + c * FFN_CHUNK)
        z = gc * _sigmoid(gc) * uc
        acc = acc + _bdot(z, wout_ref[c * FFN_CHUNK:(c + 1) * FFN_CHUNK, :])
    new_hist = buf[:, Tt + PAD - W1:Tt + PAD, :]
    st_out[...] = new_hist
    buf[:, PAD - W1:PAD, :] = new_hist
    y_ref[...] = (x + _rms(acc, gout_ref[...])).reshape(Bb, Tt, D)


def _conv_ffn(x, st, gin, gout, fw):
    B, T, D = x.shape
    Bb, Tt = _tiles(B, T, max_bb=2 * SUBLANES)
    F2 = fw['w_in'].shape[1]
    F = F2 // 2
    assert F % FFN_CHUNK == 0
    W1 = FFN_CONV_WIDTH - 1
    row = pl.BlockSpec((Bb, Tt, D), lambda b, t: (b, t, 0))
    stspec = pl.BlockSpec((Bb, W1, F2), lambda b, t: (b, 0, 0))
    wargs = [gin, fw['w_in'], fw['dw_w'], fw['dw_b'], fw['w_out'], gout]
    y, st_new = pl.pallas_call(
        functools.partial(_conv_ffn_kernel, Bb, Tt, F),
        grid=(B // Bb, T // Tt),
        in_specs=[row, stspec] + [_wspec(w.shape) for w in wargs],
        out_specs=[row, stspec],
        out_shape=[jax.ShapeDtypeStruct((B, T, D), _F32), jax.ShapeDtypeStruct((B, W1, F2), _F32)],
        scratch_shapes=[pltpu.VMEM((Bb, SUBLANES + Tt, F2), _F32)],
        compiler_params=_params(1),
        name='conv_ffn',
    )(x, st, *wargs)
    return y, st_new


def _row(p):
    return p.reshape(1, -1).astype(_F32)


def _prep_weights(p):
    n_rwkv = p['rwkv_wr'].shape[0]
    rw = []
    for j in range(n_rwkv):
        lw = {n: p['rwkv_' + n][j].astype(_BF)
              for n in ('wr', 'wk', 'wv', 'wo', 'w1', 'w2', 'a1', 'a2', 'g1', 'g2')}
        lw['mix'] = p['rwkv_mix'][j]
        for n in ('w0', 'a0'):
            lw[n] = _row(p['rwkv_' + n][j])
        for n in ('k_k', 'k_a', 'lnx_w', 'lnx_b'):
            lw[n] = p['rwkv_' + n][j]
        lw['r_k'] = p['rwkv_r_k'][j].reshape(-1)
        if j > 0:
            lw['v0'] = _row(p['rwkv_v0'][j - 1])
            lw['v1'] = p['rwkv_v1'][j - 1].astype(_BF)
            lw['v2'] = p['rwkv_v2'][j - 1].astype(_BF)
        rw.append(lw)
    cw = []
    for j in range(p['conv_w_in'].shape[0]):
        cw.append({
            'w_in': p['conv_w_in'][j].astype(_BF), 'b_in': _row(p['conv_b_in'][j]),
            'dw_w': p['conv_dw_w'][j], 'dw_b': _row(p['conv_dw_b'][j]),
            'ln_w': _row(p['conv_ln_w'][j]), 'ln_b': _row(p['conv_ln_b'][j]),
            'w_out': p['conv_w_out'][j].astype(_BF), 'b_out': _row(p['conv_b_out'][j])})
    fw = []
    for i in range(p['ffn_w_in'].shape[0]):
        fw.append({
            'w_in': p['ffn_w_in'][i].astype(_BF), 'dw_w': p['ffn_dw_w'][i],
            'dw_b': _row(p['ffn_dw_b'][i]), 'w_out': p['ffn_w_out'][i].astype(_BF)})
    return rw, cw, fw


def _trunk(x, st_wkv, st_shift, st_glu, st_ffn, gains, rw, cw, fw):
    v_first = None
    wkv_l, shift_l, glu_l, ffn_l = [], [], [], []
    for i in range(DEPTH):
        j = i // 2
        g = [_row(gains[i, n]) for n in range(4)]
        if i % 2 == 0:
            lw = rw[j]
            r, w, k, v, a, gate, sh = _rwkv_proj(x, st_shift[j], v_first, g[0], lw)
            if v_first is None:
                v_first = v
            o, s_new = _wkv(r, w, k, v, a, st_wkv[j], lw)
            x = _rwkv_out(x, o, gate, lw['wo'], g[1])
            wkv_l.append(s_new)
            shift_l.append(sh)
        else:
            x, gb = _conformer(x, st_glu[j], g[0], g[1], cw[j])
            glu_l.append(gb)
        x, fb = _conv_ffn(x, st_ffn[i], g[2], g[3], fw[i])
        ffn_l.append(fb)
    return x, jnp.stack(wkv_l), jnp.stack(shift_l), jnp.stack(glu_l), jnp.stack(ffn_l)


def kernel(x_prompt, x_sample, state_rwkv_wkv, state_rwkv_shift, state_conv_glu, state_ffn_conv, norm_gain, rwkv_mix, rwkv_wr, rwkv_wk, rwkv_wv, rwkv_wo, rwkv_w0, rwkv_w1, rwkv_w2, rwkv_a0, rwkv_a1, rwkv_a2, rwkv_v0, rwkv_v1, rwkv_v2, rwkv_g1, rwkv_g2, rwkv_k_k, rwkv_k_a, rwkv_r_k, rwkv_lnx_w, rwkv_lnx_b, conv_w_in, conv_b_in, conv_dw_w, conv_dw_b, conv_ln_w, conv_ln_b, conv_w_out, conv_b_out, ffn_w_in, ffn_dw_w, ffn_dw_b, ffn_w_out):
    p = dict(rwkv_mix=rwkv_mix, rwkv_wr=rwkv_wr, rwkv_wk=rwkv_wk, rwkv_wv=rwkv_wv,
             rwkv_wo=rwkv_wo, rwkv_w0=rwkv_w0, rwkv_w1=rwkv_w1, rwkv_w2=rwkv_w2,
             rwkv_a0=rwkv_a0, rwkv_a1=rwkv_a1, rwkv_a2=rwkv_a2, rwkv_v0=rwkv_v0,
             rwkv_v1=rwkv_v1, rwkv_v2=rwkv_v2, rwkv_g1=rwkv_g1, rwkv_g2=rwkv_g2,
             rwkv_k_k=rwkv_k_k, rwkv_k_a=rwkv_k_a, rwkv_r_k=rwkv_r_k,
             rwkv_lnx_w=rwkv_lnx_w, rwkv_lnx_b=rwkv_lnx_b, conv_w_in=conv_w_in,
             conv_b_in=conv_b_in, conv_dw_w=conv_dw_w, conv_dw_b=conv_dw_b,
             conv_ln_w=conv_ln_w, conv_ln_b=conv_ln_b, conv_w_out=conv_w_out,
             conv_b_out=conv_b_out, ffn_w_in=ffn_w_in, ffn_dw_w=ffn_dw_w,
             ffn_dw_b=ffn_dw_b, ffn_w_out=ffn_w_out)
    rw, cw, fw = _prep_weights(p)
    bp = x_prompt.shape[0]
    n_rwkv, n_conv, depth = state_rwkv_wkv.shape[0], state_conv_glu.shape[0], state_ffn_conv.shape[0]
    z_wkv = jnp.zeros((n_rwkv, bp) + state_rwkv_wkv.shape[2:], _F32)
    z_shift = jnp.zeros((n_rwkv, bp, D_MODEL), _F32)
    z_glu = jnp.zeros((n_conv, bp) + state_conv_glu.shape[2:], _F32)
    z_ffn = jnp.zeros((depth, bp) + state_ffn_conv.shape[2:], _F32)
    yp, p_wkv, p_shift, p_glu, p_ffn = _trunk(x_prompt, z_wkv, z_shift, z_glu, z_ffn,
                                              norm_gain, rw, cw, fw)
    ys, s_wkv, s_shift, s_glu, s_ffn = _trunk(x_sample, state_rwkv_wkv, state_rwkv_shift,
                                              state_conv_glu, state_ffn_conv,
                                              norm_gain, rw, cw, fw)
    wdt = state_rwkv_wkv.dtype
    return (yp, ys, p_wkv.astype(wdt), p_shift, p_glu, p_ffn,
            s_wkv.astype(wdt), s_shift, s_glu, s_ffn)
```

```python
import functools

import jax
import jax.numpy as jnp
from jax import lax
from jax.experimental import pallas as pl
from jax.experimental.pallas import tpu as pltpu

D_MODEL = 1024
HEAD_SIZE = 64
N_HEADS = D_MODEL // HEAD_SIZE
DEPTH = 4
CONV_WIDTH = 31
FFN_CONV_WIDTH = 3
NORM_EPS = 1e-6
LN_EPS = 1e-5
LNX_EPS = 64e-5

LANES = 128
SUBLANES = 8
ROWS = 256
FFN_CHUNK = 256
WKV_T_CHUNK = 32
VMEM_LIMIT = 56 * 1024 * 1024

_BF = jnp.bfloat16
_F32 = jnp.float32


def _tiles(T, B):
    Tt = min(T, ROWS // SUBLANES)
    Bb = min(B, ROWS // Tt)
    assert T % Tt == 0 and B % Bb == 0 and Bb % SUBLANES == 0, (T, B)
    return Tt, Bb


def _lspec(arr, layer):
    nd = arr.ndim - 1
    return pl.BlockSpec((None,) + arr.shape[1:], lambda *_: (layer,) + (0,) * nd,
                        pipeline_mode=pl.Buffered(1))


def _rms(x, g):
    return x * lax.rsqrt(jnp.mean(x * x, axis=-1, keepdims=True) + NORM_EPS) * g


def _bdot(a, w):
    return jnp.dot(a.astype(_BF), w, preferred_element_type=_F32)


def _sigmoid(x):
    return 1.0 / (1.0 + jnp.exp(-x))


def _params(n_par):
    return pltpu.CompilerParams(
        dimension_semantics=("parallel",) * n_par + ("arbitrary",),
        vmem_limit_bytes=VMEM_LIMIT)


def _row_spec(Tt, Bb):
    return pl.BlockSpec((Tt, Bb, D_MODEL), lambda b, t: (t, b, 0))


def _rwkv_proj_kernel(has_vres, Tt, Bb, *refs):
    refs = list(refs)
    x_ref, shift_ref = refs[:2]
    refs = refs[2:]
    if has_vres:
        vfirst_ref = refs.pop(0)
    (gain_ref, mix_ref, wr_ref, wk_ref, wv_ref, w0_ref, w1_ref, w2_ref,
     a0_ref, a1_ref, a2_ref, g1_ref, g2_ref) = refs[:13]
    refs = refs[13:]
    if has_vres:
        v0_ref, v1_ref, v2_ref = refs[:3]
        refs = refs[3:]
    r_out, w_out, k_out, v_out, a_out, g_out, shift_out, xs = refs

    R = Tt * Bb
    t = pl.program_id(1)
    h = _rms(x_ref[...].reshape(R, D_MODEL), gain_ref[0:1, :])
    h3 = h.reshape(Tt, Bb, D_MODEL)

    @pl.when(t == 0)
    def _():
        xs[0] = shift_ref[...]

    xs[1:1 + Tt] = h3
    xx = (xs[0:Tt] - h3).reshape(R, D_MODEL)
    last = h3[Tt - 1]
    xs[0] = last
    shift_out[...] = last

    def mixed(m):
        return h + xx * mix_ref[m:m + 1, :]

    def out3(v):
        return v.reshape(Tt, Bb, D_MODEL)

    xv = mixed(3)
    lw = _bdot(mixed(1), w1_ref[...])
    la = _bdot(mixed(4), a1_ref[...])
    lg = _bdot(mixed(5), g1_ref[...])
    if has_vres:
        lv = _bdot(xv, v1_ref[...])
    r_out[...] = out3(_bdot(mixed(0), wr_ref[...]))
    k_out[...] = out3(_bdot(mixed(2), wk_ref[...]))
    v = _bdot(xv, wv_ref[...])
    wl = w0_ref[...] + _bdot(jnp.tanh(lw), w2_ref[...])
    w_log = -(jnp.maximum(-wl, 0.0) + jnp.log(1.0 + jnp.exp(-jnp.abs(wl)))) - 0.5
    w_out[...] = out3(jnp.exp(-jnp.exp(w_log)))
    a_out[...] = out3(_sigmoid(a0_ref[...] + _bdot(la, a2_ref[...])))
    g_out[...] = out3(_bdot(_sigmoid(lg), g2_ref[...]))
    if has_vres:
        vf = vfirst_ref[...].reshape(R, D_MODEL)
        v = v + (vf - v) * _sigmoid(v0_ref[...] + _bdot(lv, v2_ref[...]))
    v_out[...] = out3(v)


def _rwkv_proj(x, shift, v_first, W, i, j):
    T, B, D = x.shape
    Tt, Bb = _tiles(T, B)
    has_vres = v_first is not None
    row = _row_spec(Tt, Bb)
    st = pl.BlockSpec((Bb, D), lambda b, t: (b, 0))
    args = [x, shift]
    specs = [row, st]
    if has_vres:
        args.append(v_first)
        specs.append(row)
    names = ['rwkv_mix', 'rwkv_wr', 'rwkv_wk', 'rwkv_wv', 'rwkv_w0', 'rwkv_w1', 'rwkv_w2',
             'rwkv_a0', 'rwkv_a1', 'rwkv_a2', 'rwkv_g1', 'rwkv_g2']
    args.append(W['norm_gain'])
    specs.append(_lspec(W['norm_gain'], i))
    for n in names:
        args.append(W[n])
        specs.append(_lspec(W[n], j))
    if has_vres:
        for n in ('rwkv_v0', 'rwkv_v1', 'rwkv_v2'):
            args.append(W[n])
            specs.append(_lspec(W[n], j - 1))
    act = jax.ShapeDtypeStruct((T, B, D), _F32)
    outs = pl.pallas_call(
        functools.partial(_rwkv_proj_kernel, has_vres, Tt, Bb),
        grid=(B // Bb, T // Tt),
        in_specs=specs,
        out_specs=[row] * 6 + [st],
        out_shape=[act] * 6 + [jax.ShapeDtypeStruct((B, D), _F32)],
        scratch_shapes=[pltpu.VMEM((1 + Tt, Bb, D), _F32)],
        compiler_params=_params(1),
        name='rwkv_proj_vres' if has_vres else 'rwkv_proj',
    )(*args)
    return outs


def _wkv_kernel(Tc, r_ref, w_ref, k_ref, v_ref, a_ref, kk_ref, ka_ref, rk_ref, lnw_ref,
                lnb_ref, s0_ref, o_ref, s_ref, an_s, bn_s, km_s):
    N = HEAD_SIZE

    @pl.when(pl.program_id(1) == 0)
    def _():
        s_ref[...] = s0_ref[...]

    k = k_ref[...]
    ag = a_ref[...]
    kk = k * kk_ref[...]
    kk = kk / jnp.maximum(jnp.sqrt(jnp.sum(kk * kk, axis=1, keepdims=True)), 1e-12)
    km = k * (1.0 + (ag - 1.0) * ka_ref[...])
    an_s[...] = -kk
    bn_s[...] = kk * ag
    km_s[...] = km

    def row(ref, t, j):
        return ref[t, pl.ds(j, 1), :]

    sa = jnp.zeros((N, LANES), _F32)
    for j in range(N):
        sa = sa + s_ref[j] * row(an_s, 0, j)

    def step(t, sa):
        tn = jnp.minimum(t + 1, Tc - 1)
        v = v_ref[t]
        y = jnp.zeros((N, LANES), _F32)
        sa_next = jnp.zeros((N, LANES), _F32)
        for j in range(N):
            sn = s_ref[j] * row(w_ref, t, j) + sa * row(bn_s, t, j) + v * row(km_s, t, j)
            s_ref[j] = sn
            y = y + sn * row(r_ref, t, j)
            sa_next = sa_next + sn * row(an_s, tn, j)
        o_ref[t] = y
        return sa_next

    lax.fori_loop(0, Tc, step, sa)

    y = o_ref[...]
    mu = jnp.mean(y, axis=1, keepdims=True)
    d = y - mu
    var = jnp.mean(d * d, axis=1, keepdims=True)
    yn = d * lax.rsqrt(var + LNX_EPS) * lnw_ref[...] + lnb_ref[...]
    bonus = jnp.sum(r_ref[...] * km * rk_ref[...], axis=1, keepdims=True) * v_ref[...]
    o_ref[...] = yn + bonus


def _to_scan(x):
    T, B, _ = x.shape
    return x.reshape(T, B * N_HEADS, HEAD_SIZE).swapaxes(1, 2)


def _from_scan(o, B):
    T = o.shape[0]
    return o.swapaxes(1, 2).reshape(T, B, D_MODEL)


def _lane_param(p):
    return jnp.tile(p.reshape(N_HEADS, HEAD_SIZE).T, (1, LANES // N_HEADS))


def _wkv(r, w, k, v, a, s0, W, j):
    T, B, _ = r.shape
    L = B * N_HEADS
    assert L % LANES == 0
    Tc = min(T, WKV_T_CHUNK)
    assert T % Tc == 0
    N = HEAD_SIZE
    seq = [_to_scan(z) for z in (r, w, k, v, a)]
    s0t = s0.astype(_F32).transpose(3, 2, 0, 1).reshape(N, N, L)
    pars = [_lane_param(W[n][j].reshape(-1))
            for n in ('rwkv_k_k', 'rwkv_k_a', 'rwkv_r_k', 'rwkv_lnx_w', 'rwkv_lnx_b')]
    seq_spec = pl.BlockSpec((Tc, N, LANES), lambda l, t: (t, 0, l))
    st_spec = pl.BlockSpec((N, N, LANES), lambda l, t: (0, 0, l))
    par_spec = pl.BlockSpec((N, LANES), lambda l, t: (0, 0))
    o, sT = pl.pallas_call(
        functools.partial(_wkv_kernel, Tc),
        grid=(L // LANES, T // Tc),
        in_specs=[seq_spec] * 5 + [par_spec] * 5 + [st_spec],
        out_specs=[seq_spec, st_spec],
        out_shape=[jax.ShapeDtypeStruct((T, N, L), _F32), jax.ShapeDtypeStruct((N, N, L), _F32)],
        scratch_shapes=[pltpu.VMEM((Tc, N, LANES), _F32)] * 3,
        compiler_params=_params(1),
        name='wkv_scan',
    )(*seq, *pars, s0t)
    s_new = sT.reshape(N, N, B, N_HEADS).transpose(2, 3, 1, 0)
    return _from_scan(o, B), s_new


def _rwkv_out_kernel(Tt, Bb, x_ref, o_ref, g_ref, wo_ref, gain_ref, y_ref):
    R = Tt * Bb
    m = _bdot((o_ref[...] * g_ref[...]).reshape(R, D_MODEL), wo_ref[...])
    y_ref[...] = x_ref[...] + _rms(m, gain_ref[1:2, :]).reshape(Tt, Bb, D_MODEL)


def _rwkv_out(x, o, g, W, i, j):
    T, B, D = x.shape
    Tt, Bb = _tiles(T, B)
    row = _row_spec(Tt, Bb)
    return pl.pallas_call(
        functools.partial(_rwkv_out_kernel, Tt, Bb),
        grid=(B // Bb, T // Tt),
        in_specs=[row, row, row, _lspec(W['rwkv_wo'], j), _lspec(W['norm_gain'], i)],
        out_specs=row,
        out_shape=jax.ShapeDtypeStruct((T, B, D), _F32),
        compiler_params=_params(1),
        name='rwkv_out',
    )(x, o, g, W['rwkv_wo'], W['norm_gain'])


def _conformer_kernel(Tt, Bb, x_ref, st_ref, gain_ref, win_ref, bin_ref, dww_ref, dwb_ref,
                      lnw_ref, lnb_ref, wout_ref, bout_ref, y_ref, st_out, buf, cbuf):
    R = Tt * Bb
    D = D_MODEL
    W1 = CONV_WIDTH - 1
    t = pl.program_id(1)
    x = x_ref[...].reshape(R, D)
    hh = _bdot(_rms(x, gain_ref[0:1, :]), win_ref[...]) + bin_ref[...]
    glu = hh[:, :D] * _sigmoid(hh[:, D:])

    @pl.when(t == 0)
    def _():
        buf[0:W1] = st_ref[...]

    buf[W1:W1 + Tt] = glu.reshape(Tt, Bb, D)
    for c in range(D // LANES):
        cs = slice(c * LANES, (c + 1) * LANES)
        acc = jnp.zeros((Tt, Bb, LANES), _F32) + dwb_ref[:, cs]
        for kk in range(CONV_WIDTH):
            acc = acc + buf[kk:kk + Tt, :, cs] * dww_ref[kk:kk + 1, cs]
        cbuf[:, :, cs] = acc
    new_hist = buf[Tt:Tt + W1]
    st_out[...] = new_hist
    buf[0:W1] = new_hist

    cv = cbuf[...].reshape(R, D)
    mu = jnp.mean(cv, axis=-1, keepdims=True)
    d = cv - mu
    var = jnp.mean(d * d, axis=-1, keepdims=True)
    ln = d * lax.rsqrt(var + LN_EPS) * lnw_ref[...] + lnb_ref[...]
    m = _bdot(ln * _sigmoid(ln), wout_ref[...]) + bout_ref[...]
    y_ref[...] = (x + _rms(m, gain_ref[1:2, :])).reshape(Tt, Bb, D)


def _conformer(x, st, W, i, j):
    T, B, D = x.shape
    Tt, Bb = _tiles(T, B)
    W1 = CONV_WIDTH - 1
    row = _row_spec(Tt, Bb)
    stspec = pl.BlockSpec((W1, Bb, D), lambda b, t: (0, b, 0))
    names = ['conv_w_in', 'conv_b_in', 'conv_dw_w', 'conv_dw_b', 'conv_ln_w', 'conv_ln_b',
             'conv_w_out', 'conv_b_out']
    y, st_new = pl.pallas_call(
        functools.partial(_conformer_kernel, Tt, Bb),
        grid=(B // Bb, T // Tt),
        in_specs=[row, stspec, _lspec(W['norm_gain'], i)] + [_lspec(W[n], j) for n in names],
        out_specs=[row, stspec],
        out_shape=[jax.ShapeDtypeStruct((T, B, D), _F32), jax.ShapeDtypeStruct((W1, B, D), _F32)],
        scratch_shapes=[pltpu.VMEM((W1 + Tt, Bb, D), _F32), pltpu.VMEM((Tt, Bb, D), _F32)],
        compiler_params=_params(1),
        name='conformer_conv',
    )(x, st, W['norm_gain'], *[W[n] for n in names])
    return y, st_new


def _conv_ffn_kernel(Tt, Bb, F, x_ref, st_ref, gain_ref, win_ref, dww_ref, dwb_ref, wout_ref,
                     y_ref, st_out, buf):
    R = Tt * Bb
    D = D_MODEL
    W1 = FFN_CONV_WIDTH - 1
    C = FFN_CHUNK
    t = pl.program_id(1)
    x = x_ref[...].reshape(R, D)
    hb = _rms(x, gain_ref[2:3, :]).astype(_BF)

    @pl.when(t == 0)
    def _():
        buf[0:W1] = st_ref[...]

    def up(c):
        for col0 in (c * C, F + c * C):
            cs = slice(col0, col0 + C)
            u = jnp.dot(hb, win_ref[:, cs], preferred_element_type=_F32)
            buf[W1:W1 + Tt, :, cs] = u.reshape(Tt, Bb, C)

    def conv(col0):
        cs = slice(col0, col0 + C)
        acc = dwb_ref[:, cs] + buf[W1:W1 + Tt, :, cs] * dww_ref[W1:W1 + 1, cs]
        for kk in range(W1):
            acc = acc + buf[kk:kk + Tt, :, cs] * dww_ref[kk:kk + 1, cs]
        return acc.reshape(R, C)

    n_chunks = F // C
    acc = jnp.zeros((R, D), _F32)
    up(0)
    for c in range(n_chunks):
        if c + 1 < n_chunks:
            up(c + 1)
        gc = conv(c * C)
        uc = conv(F + c * C)
        z = gc * _sigmoid(gc) * uc
        acc = acc + _bdot(z, wout_ref[c * C:(c + 1) * C, :])
    new_hist = buf[Tt:Tt + W1]
    st_out[...] = new_hist
    buf[0:W1] = new_hist
    y_ref[...] = (x + _rms(acc, gain_ref[3:4, :])).reshape(Tt, Bb, D)


def _conv_ffn(x, st, W, i):
    T, B, D = x.shape
    Tt, Bb = _tiles(T, B)
    F2 = W['ffn_w_in'].shape[-1]
    F = F2 // 2
    assert F % FFN_CHUNK == 0
    W1 = FFN_CONV_WIDTH - 1
    row = _row_spec(Tt, Bb)
    stspec = pl.BlockSpec((W1, Bb, F2), lambda b, t: (0, b, 0))
    names = ['ffn_w_in', 'ffn_dw_w', 'ffn_dw_b', 'ffn_w_out']
    y, st_new = pl.pallas_call(
        functools.partial(_conv_ffn_kernel, Tt, Bb, F),
        grid=(B // Bb, T // Tt),
        in_specs=[row, stspec, _lspec(W['norm_gain'], i)] + [_lspec(W[n], i) for n in names],
        out_specs=[row, stspec],
        out_shape=[jax.ShapeDtypeStruct((T, B, D), _F32), jax.ShapeDtypeStruct((W1, B, F2), _F32)],
        scratch_shapes=[pltpu.VMEM((W1 + Tt, Bb, F2), _F32)],
        compiler_params=_params(1),
        name='conv_ffn',
    )(x, st, W['norm_gain'], *[W[n] for n in names])
    return y, st_new


_MATRICES = ('rwkv_wr', 'rwkv_wk', 'rwkv_wv', 'rwkv_wo', 'rwkv_w1', 'rwkv_w2', 'rwkv_a1',
             'rwkv_a2', 'rwkv_v1', 'rwkv_v2', 'rwkv_g1', 'rwkv_g2', 'conv_w_in', 'conv_w_out',
             'ffn_w_in', 'ffn_w_out')
_VECTORS = ('rwkv_w0', 'rwkv_a0', 'rwkv_v0', 'conv_b_in', 'conv_dw_b', 'conv_ln_w', 'conv_ln_b',
            'conv_b_out', 'ffn_dw_b')


def _prep_weights(p):
    W = dict(p)
    for n in _MATRICES:
        W[n] = p[n].astype(_BF)
    for n in _VECTORS:
        W[n] = p[n].reshape(p[n].shape[0], 1, p[n].shape[1])
    return W


def _trunk(x, st_wkv, st_shift, st_glu, st_ffn, W):
    x = x.swapaxes(0, 1)
    v_first = None
    wkv_l, shift_l, glu_l, ffn_l = [], [], [], []
    for i in range(DEPTH):
        j = i // 2
        if i % 2 == 0:
            r, w, k, v, a, gate, sh = _rwkv_proj(x, st_shift[j], v_first, W, i, j)
            if v_first is None:
                v_first = v
            o, s_new = _wkv(r, w, k, v, a, st_wkv[j], W, j)
            x = _rwkv_out(x, o, gate, W, i, j)
            wkv_l.append(s_new)
            shift_l.append(sh)
        else:
            x, gb = _conformer(x, st_glu[j].swapaxes(0, 1), W, i, j)
            glu_l.append(gb.swapaxes(0, 1))
        x, fb = _conv_ffn(x, st_ffn[i].swapaxes(0, 1), W, i)
        ffn_l.append(fb.swapaxes(0, 1))
    return (x.swapaxes(0, 1), jnp.stack(wkv_l), jnp.stack(shift_l), jnp.stack(glu_l),
            jnp.stack(ffn_l))


def kernel(x_prompt, x_sample, state_rwkv_wkv, state_rwkv_shift, state_conv_glu, state_ffn_conv, norm_gain, rwkv_mix, rwkv_wr, rwkv_wk, rwkv_wv, rwkv_wo, rwkv_w0, rwkv_w1, rwkv_w2, rwkv_a0, rwkv_a1, rwkv_a2, rwkv_v0, rwkv_v1, rwkv_v2, rwkv_g1, rwkv_g2, rwkv_k_k, rwkv_k_a, rwkv_r_k, rwkv_lnx_w, rwkv_lnx_b, conv_w_in, conv_b_in, conv_dw_w, conv_dw_b, conv_ln_w, conv_ln_b, conv_w_out, conv_b_out, ffn_w_in, ffn_dw_w, ffn_dw_b, ffn_w_out):
    p = dict(norm_gain=norm_gain, rwkv_mix=rwkv_mix, rwkv_wr=rwkv_wr, rwkv_wk=rwkv_wk,
             rwkv_wv=rwkv_wv, rwkv_wo=rwkv_wo, rwkv_w0=rwkv_w0, rwkv_w1=rwkv_w1,
             rwkv_w2=rwkv_w2, rwkv_a0=rwkv_a0, rwkv_a1=rwkv_a1, rwkv_a2=rwkv_a2,
             rwkv_v0=rwkv_v0, rwkv_v1=rwkv_v1, rwkv_v2=rwkv_v2, rwkv_g1=rwkv_g1,
             rwkv_g2=rwkv_g2, rwkv_k_k=rwkv_k_k, rwkv_k_a=rwkv_k_a, rwkv_r_k=rwkv_r_k,
             rwkv_lnx_w=rwkv_lnx_w, rwkv_lnx_b=rwkv_lnx_b, conv_w_in=conv_w_in,
             conv_b_in=conv_b_in, conv_dw_w=conv_dw_w, conv_dw_b=conv_dw_b,
             conv_ln_w=conv_ln_w, conv_ln_b=conv_ln_b, conv_w_out=conv_w_out,
             conv_b_out=conv_b_out, ffn_w_in=ffn_w_in, ffn_dw_w=ffn_dw_w,
             ffn_dw_b=ffn_dw_b, ffn_w_out=ffn_w_out)
    W = _prep_weights(p)
    bp = x_prompt.shape[0]
    n_rwkv, n_conv, depth = state_rwkv_wkv.shape[0], state_conv_glu.shape[0], state_ffn_conv.shape[0]
    z_wkv = jnp.zeros((n_rwkv, bp) + state_rwkv_wkv.shape[2:], _F32)
    z_shift = jnp.zeros((n_rwkv, bp, D_MODEL), _F32)
    z_glu = jnp.zeros((n_conv, bp) + state_conv_glu.shape[2:], _F32)
    z_ffn = jnp.zeros((depth, bp) + state_ffn_conv.shape[2:], _F32)
    yp, p_wkv, p_shift, p_glu, p_ffn = _trunk(x_prompt, z_wkv, z_shift, z_glu, z_ffn, W)
    ys, s_wkv, s_shift, s_glu, s_ffn = _trunk(x_sample, state_rwkv_wkv, state_rwkv_shift,
                                              state_conv_glu, state_ffn_conv, W)
    wdt = state_rwkv_wkv.dtype
    return (yp, ys, p_wkv.astype(wdt), p_shift, p_glu, p_ffn,
            s_wkv.astype(wdt), s_shift, s_glu, s_ffn)
```

```python
import functools

import jax
import jax.numpy as jnp
from jax import lax
from jax.experimental import pallas as pl
from jax.experimental.pallas import tpu as pltpu

D_MODEL = 1024
HEAD_SIZE = 64
N_HEADS = D_MODEL // HEAD_SIZE
DEPTH = 4
CONV_WIDTH = 31
FFN_CONV_WIDTH = 3
NORM_EPS = 1e-6
LN_EPS = 1e-5
LNX_EPS = 64e-5

LANES = 128
SUBLANES = 8
ROWS = 256
FFN_CHUNK = 256
WKV_T_CHUNK = 32
VMEM_LIMIT = 56 * 1024 * 1024

_BF = jnp.bfloat16
_F32 = jnp.float32


def _tiles(T, B):
    Tt = min(T, ROWS // SUBLANES)
    Bb = min(B, ROWS // Tt)
    assert T % Tt == 0 and B % Bb == 0 and Bb % SUBLANES == 0, (T, B)
    return Tt, Bb


def _lspec(arr, layer):
    nd = arr.ndim - 1
    return pl.BlockSpec((None,) + arr.shape[1:], lambda *_: (layer,) + (0,) * nd,
                        pipeline_mode=pl.Buffered(1))


def _rms(x, g):
    return x * lax.rsqrt(jnp.mean(x * x, axis=-1, keepdims=True) + NORM_EPS) * g


def _bdot(a, w):
    return jnp.dot(a.astype(_BF), w, preferred_element_type=_F32)


def _sigmoid(x):
    return 1.0 / (1.0 + jnp.exp(-x))


def _params(n_par):
    return pltpu.CompilerParams(
        dimension_semantics=("parallel",) * n_par + ("arbitrary",),
        vmem_limit_bytes=VMEM_LIMIT)


def _row_spec(Tt, Bb):
    return pl.BlockSpec((Tt, Bb, D_MODEL), lambda b, t: (t, b, 0))


def _rwkv_proj_kernel(has_vres, Tt, Bb, *refs):
    refs = list(refs)
    x_ref, shift_ref = refs[:2]
    refs = refs[2:]
    if has_vres:
        vfirst_ref = refs.pop(0)
    (gain_ref, mix_ref, wr_ref, wk_ref, wv_ref, w0_ref, w1_ref, w2_ref,
     a0_ref, a1_ref, a2_ref, g1_ref, g2_ref) = refs[:13]
    refs = refs[13:]
    if has_vres:
        v0_ref, v1_ref, v2_ref = refs[:3]
        refs = refs[3:]
    r_out, w_out, k_out, v_out, a_out, g_out, shift_out, xs = refs

    R = Tt * Bb
    t = pl.program_id(1)
    h = _rms(x_ref[...].reshape(R, D_MODEL), gain_ref[0:1, :])
    h3 = h.reshape(Tt, Bb, D_MODEL)

    @pl.when(t == 0)
    def _():
        xs[0] = shift_ref[...]

    xs[1:1 + Tt] = h3
    xx = (xs[0:Tt] - h3).reshape(R, D_MODEL)
    last = h3[Tt - 1]
    xs[0] = last
    shift_out[...] = last

    def mixed(m):
        return h + xx * mix_ref[m:m + 1, :]

    def out3(v):
        return v.reshape(Tt, Bb, D_MODEL)

    xv = mixed(3)
    lw = _bdot(mixed(1), w1_ref[...])
    la = _bdot(mixed(4), a1_ref[...])
    lg = _bdot(mixed(5), g1_ref[...])
    if has_vres:
        lv = _bdot(xv, v1_ref[...])
    r_out[...] = out3(_bdot(mixed(0), wr_ref[...]))
    k_out[...] = out3(_bdot(mixed(2), wk_ref[...]))
    v = _bdot(xv, wv_ref[...])
    wl = w0_ref[...] + _bdot(jnp.tanh(lw), w2_ref[...])
    w_log = -(jnp.maximum(-wl, 0.0) + jnp.log(1.0 + jnp.exp(-jnp.abs(wl)))) - 0.5
    w_out[...] = out3(jnp.exp(-jnp.exp(w_log)))
    a_out[...] = out3(_sigmoid(a0_ref[...] + _bdot(la, a2_ref[...])))
    g_out[...] = out3(_bdot(_sigmoid(lg), g2_ref[...]))
    if has_vres:
        vf = vfirst_ref[...].reshape(R, D_MODEL)
        v = v + (vf - v) * _sigmoid(v0_ref[...] + _bdot(lv, v2_ref[...]))
    v_out[...] = out3(v)


def _rwkv_proj(x, shift, v_first, W, i, j):
    T, B, D = x.shape
    Tt, Bb = _tiles(T, B)
    has_vres = v_first is not None
    row = _row_spec(Tt, Bb)
    st = pl.BlockSpec((Bb, D), lambda b, t: (b, 0))
    args = [x, shift]
    specs = [row, st]
    if has_vres:
        args.append(v_first)
        specs.append(row)
    names = ['rwkv_mix', 'rwkv_wr', 'rwkv_wk', 'rwkv_wv', 'rwkv_w0', 'rwkv_w1', 'rwkv_w2',
             'rwkv_a0', 'rwkv_a1', 'rwkv_a2', 'rwkv_g1', 'rwkv_g2']
    args.append(W['norm_gain'])
    specs.append(_lspec(W['norm_gain'], i))
    for n in names:
        args.append(W[n])
        specs.append(_lspec(W[n], j))
    if has_vres:
        for n in ('rwkv_v0', 'rwkv_v1', 'rwkv_v2'):
            args.append(W[n])
            specs.append(_lspec(W[n], j - 1))
    act = jax.ShapeDtypeStruct((T, B, D), _F32)
    outs = pl.pallas_call(
        functools.partial(_rwkv_proj_kernel, has_vres, Tt, Bb),
        grid=(B // Bb, T // Tt),
        in_specs=specs,
        out_specs=[row] * 6 + [st],
        out_shape=[act] * 6 + [jax.ShapeDtypeStruct((B, D), _F32)],
        scratch_shapes=[pltpu.VMEM((1 + Tt, Bb, D), _F32)],
        compiler_params=_params(1),
        name='rwkv_proj_vres' if has_vres else 'rwkv_proj',
    )(*args)
    return outs


def _head_halves(a, b, low):
    return (jnp.where(low, a, pltpu.roll(b, HEAD_SIZE, axis=1)),
            jnp.where(low, pltpu.roll(a, HEAD_SIZE, axis=1), b))


def _to_lanes(src_ref, dst_ref, Tc):
    low = lax.broadcasted_iota(jnp.int32, (SUBLANES, LANES), 1) < HEAD_SIZE
    for q in range(Tc // 2):
        rows = []
        for p in range(D_MODEL // LANES):
            cs = slice(p * LANES, (p + 1) * LANES)
            rows.extend(_head_halves(src_ref[2 * q, :, cs], src_ref[2 * q + 1, :, cs], low))
        z = jnp.concatenate(rows, axis=0).T
        dst_ref[2 * q] = z[:HEAD_SIZE]
        dst_ref[2 * q + 1] = z[HEAD_SIZE:]


def _from_lanes(src_ref, dst_ref, Tc):
    low = lax.broadcasted_iota(jnp.int32, (SUBLANES, LANES), 1) < HEAD_SIZE
    for q in range(Tc // 2):
        z = jnp.concatenate([src_ref[2 * q], src_ref[2 * q + 1]], axis=0).T
        for p in range(D_MODEL // LANES):
            cs = slice(p * LANES, (p + 1) * LANES)
            even = z[(2 * p) * SUBLANES:(2 * p + 1) * SUBLANES]
            odd = z[(2 * p + 1) * SUBLANES:(2 * p + 2) * SUBLANES]
            dst_ref[2 * q, :, cs] = jnp.where(low, even, pltpu.roll(odd, HEAD_SIZE, axis=1))
            dst_ref[2 * q + 1, :, cs] = jnp.where(low, pltpu.roll(even, HEAD_SIZE, axis=1), odd)


def _wkv_kernel(Tc, r_ref, w_ref, k_ref, v_ref, a_ref, kk_ref, ka_ref, rk_ref, lnw_ref,
                lnb_ref, s0_ref, o_ref, s_ref, r_s, w_s, v_s, an_s, bn_s, km_s, o_s):
    N = HEAD_SIZE

    @pl.when(pl.program_id(1) == 0)
    def _():
        s_ref[...] = s0_ref[...]

    _to_lanes(r_ref, r_s, Tc)
    _to_lanes(w_ref, w_s, Tc)
    _to_lanes(v_ref, v_s, Tc)
    _to_lanes(k_ref, km_s, Tc)
    _to_lanes(a_ref, bn_s, Tc)

    k = km_s[...]
    ag = bn_s[...]
    kk = k * kk_ref[...]
    kk = kk / jnp.maximum(jnp.sqrt(jnp.sum(kk * kk, axis=1, keepdims=True)), 1e-12)
    km = k * (1.0 + (ag - 1.0) * ka_ref[...])
    an_s[...] = -kk
    bn_s[...] = kk * ag
    km_s[...] = km

    def row(ref, t, j):
        return ref[t, pl.ds(j, 1), :]

    sa = jnp.zeros((N, LANES), _F32)
    for j in range(N):
        sa = sa + s_ref[j] * row(an_s, 0, j)

    def step(t, sa):
        tn = jnp.minimum(t + 1, Tc - 1)
        v = v_s[t]
        y = jnp.zeros((N, LANES), _F32)
        sa_next = jnp.zeros((N, LANES), _F32)
        for j in range(N):
            sn = s_ref[j] * row(w_s, t, j) + sa * row(bn_s, t, j) + v * row(km_s, t, j)
            s_ref[j] = sn
            y = y + sn * row(r_s, t, j)
            sa_next = sa_next + sn * row(an_s, tn, j)
        o_s[t] = y
        return sa_next

    lax.fori_loop(0, Tc, step, sa)

    y = o_s[...]
    mu = jnp.mean(y, axis=1, keepdims=True)
    d = y - mu
    var = jnp.mean(d * d, axis=1, keepdims=True)
    yn = d * lax.rsqrt(var + LNX_EPS) * lnw_ref[...] + lnb_ref[...]
    bonus = jnp.sum(r_s[...] * km_s[...] * rk_ref[...], axis=1, keepdims=True) * v_s[...]
    o_s[...] = yn + bonus
    _from_lanes(o_s, o_ref, Tc)


def _lane_param(p):
    return jnp.repeat(p.reshape(N_HEADS, HEAD_SIZE).T, LANES // N_HEADS, axis=1)


def _wkv(r, w, k, v, a, s0, W, j):
    T, B, D = r.shape
    G = B // SUBLANES
    assert B % SUBLANES == 0 and N_HEADS * SUBLANES == LANES
    Tc = min(T, WKV_T_CHUNK)
    assert T % Tc == 0 and Tc % 2 == 0
    N = HEAD_SIZE
    s0t = (s0.astype(_F32).reshape(G, SUBLANES, N_HEADS, N, N).transpose(4, 3, 0, 2, 1)
           .reshape(N, N, G * LANES))
    pars = [_lane_param(W[n][j].reshape(-1))
            for n in ('rwkv_k_k', 'rwkv_k_a', 'rwkv_r_k', 'rwkv_lnx_w', 'rwkv_lnx_b')]
    seq_spec = pl.BlockSpec((Tc, SUBLANES, D), lambda g, t: (t, g, 0))
    st_spec = pl.BlockSpec((N, N, LANES), lambda g, t: (0, 0, g))
    par_spec = pl.BlockSpec((N, LANES), lambda g, t: (0, 0))
    o, sT = pl.pallas_call(
        functools.partial(_wkv_kernel, Tc),
        grid=(G, T // Tc),
        in_specs=[seq_spec] * 5 + [par_spec] * 5 + [st_spec],
        out_specs=[seq_spec, st_spec],
        out_shape=[jax.ShapeDtypeStruct((T, B, D), _F32),
                   jax.ShapeDtypeStruct((N, N, G * LANES), _F32)],
        scratch_shapes=[pltpu.VMEM((Tc, N, LANES), _F32)] * 7,
        compiler_params=_params(1),
        name='wkv_scan',
    )(r, w, k, v, a, *pars, s0t)
    s_new = (sT.reshape(N, N, G, N_HEADS, SUBLANES).transpose(2, 4, 3, 1, 0)
             .reshape(B, N_HEADS, N, N))
    return o, s_new


def _rwkv_out_kernel(Tt, Bb, x_ref, o_ref, g_ref, wo_ref, gain_ref, y_ref):
    R = Tt * Bb
    m = _bdot((o_ref[...] * g_ref[...]).reshape(R, D_MODEL), wo_ref[...])
    y_ref[...] = x_ref[...] + _rms(m, gain_ref[1:2, :]).reshape(Tt, Bb, D_MODEL)


def _rwkv_out(x, o, g, W, i, j):
    T, B, D = x.shape
    Tt, Bb = _tiles(T, B)
    row = _row_spec(Tt, Bb)
    return pl.pallas_call(
        functools.partial(_rwkv_out_kernel, Tt, Bb),
        grid=(B // Bb, T // Tt),
        in_specs=[row, row, row, _lspec(W['rwkv_wo'], j), _lspec(W['norm_gain'], i)],
        out_specs=row,
        out_shape=jax.ShapeDtypeStruct((T, B, D), _F32),
        compiler_params=_params(1),
        name='rwkv_out',
    )(x, o, g, W['rwkv_wo'], W['norm_gain'])


def _conformer_kernel(Tt, Bb, x_ref, st_ref, gain_ref, win_ref, bin_ref, dww_ref, dwb_ref,
                      lnw_ref, lnb_ref, wout_ref, bout_ref, y_ref, st_out, buf, cbuf):
    R = Tt * Bb
    D = D_MODEL
    W1 = CONV_WIDTH - 1
    t = pl.program_id(1)
    x = x_ref[...].reshape(R, D)
    hh = _bdot(_rms(x, gain_ref[0:1, :]), win_ref[...]) + bin_ref[...]
    glu = hh[:, :D] * _sigmoid(hh[:, D:])

    @pl.when(t == 0)
    def _():
        buf[0:W1] = st_ref[...]

    buf[W1:W1 + Tt] = glu.reshape(Tt, Bb, D)
    for c in range(D // LANES):
        cs = slice(c * LANES, (c + 1) * LANES)
        acc = jnp.zeros((Tt, Bb, LANES), _F32) + dwb_ref[:, cs]
        for kk in range(CONV_WIDTH):
            acc = acc + buf[kk:kk + Tt, :, cs] * dww_ref[kk:kk + 1, cs]
        cbuf[:, :, cs] = acc
    new_hist = buf[Tt:Tt + W1]
    st_out[...] = new_hist
    buf[0:W1] = new_hist

    cv = cbuf[...].reshape(R, D)
    mu = jnp.mean(cv, axis=-1, keepdims=True)
    d = cv - mu
    var = jnp.mean(d * d, axis=-1, keepdims=True)
    ln = d * lax.rsqrt(var + LN_EPS) * lnw_ref[...] + lnb_ref[...]
    m = _bdot(ln * _sigmoid(ln), wout_ref[...]) + bout_ref[...]
    y_ref[...] = (x + _rms(m, gain_ref[1:2, :])).reshape(Tt, Bb, D)


def _conformer(x, st, W, i, j):
    T, B, D = x.shape
    Tt, Bb = _tiles(T, B)
    W1 = CONV_WIDTH - 1
    row = _row_spec(Tt, Bb)
    stspec = pl.BlockSpec((W1, Bb, D), lambda b, t: (0, b, 0))
    names = ['conv_w_in', 'conv_b_in', 'conv_dw_w', 'conv_dw_b', 'conv_ln_w', 'conv_ln_b',
             'conv_w_out', 'conv_b_out']
    y, st_new = pl.pallas_call(
        functools.partial(_conformer_kernel, Tt, Bb),
        grid=(B // Bb, T // Tt),
        in_specs=[row, stspec, _lspec(W['norm_gain'], i)] + [_lspec(W[n], j) for n in names],
        out_specs=[row, stspec],
        out_shape=[jax.ShapeDtypeStruct((T, B, D), _F32), jax.ShapeDtypeStruct((W1, B, D), _F32)],
        scratch_shapes=[pltpu.VMEM((W1 + Tt, Bb, D), _F32), pltpu.VMEM((Tt, Bb, D), _F32)],
        compiler_params=_params(1),
        name='conformer_conv',
    )(x, st, W['norm_gain'], *[W[n] for n in names])
    return y, st_new


def _conv_ffn_kernel(Tt, Bb, F, x_ref, st_ref, gain_ref, win_ref, dww_ref, dwb_ref, wout_ref,
                     y_ref, st_out, buf):
    R = Tt * Bb
    D = D_MODEL
    W1 = FFN_CONV_WIDTH - 1
    C = FFN_CHUNK
    t = pl.program_id(1)
    x = x_ref[...].reshape(R, D)
    hb = _rms(x, gain_ref[2:3, :]).astype(_BF)

    @pl.when(t == 0)
    def _():
        buf[0:W1] = st_ref[...]

    def up(c):
        for col0 in (c * C, F + c * C):
            cs = slice(col0, col0 + C)
            u = jnp.dot(hb, win_ref[:, cs], preferred_element_type=_F32)
            buf[W1:W1 + Tt, :, cs] = u.reshape(Tt, Bb, C)

    def conv(col0):
        cs = slice(col0, col0 + C)
        acc = dwb_ref[:, cs] + buf[W1:W1 + Tt, :, cs] * dww_ref[W1:W1 + 1, cs]
        for kk in range(W1):
            acc = acc + buf[kk:kk + Tt, :, cs] * dww_ref[kk:kk + 1, cs]
        return acc.reshape(R, C)

    n_chunks = F // C
    acc = jnp.zeros((R, D), _F32)
    up(0)
    for c in range(n_chunks):
        if c + 1 < n_chunks:
            up(c + 1)
        gc = conv(c * C)
        uc = conv(F + c * C)
        z = gc * _sigmoid(gc) * uc
        acc = acc + _bdot(z, wout_ref[c * C:(c + 1) * C, :])
    new_hist = buf[Tt:Tt + W1]
    st_out[...] = new_hist
    buf[0:W1] = new_hist
    y_ref[...] = (x + _rms(acc, gain_ref[3:4, :])).reshape(Tt, Bb, D)


def _conv_ffn(x, st, W, i):
    T, B, D = x.shape
    Tt, Bb = _tiles(T, B)
    F2 = W['ffn_w_in'].shape[-1]
    F = F2 // 2
    assert F % FFN_CHUNK == 0
    W1 = FFN_CONV_WIDTH - 1
    row = _row_spec(Tt, Bb)
    stspec = pl.BlockSpec((W1, Bb, F2), lambda b, t: (0, b, 0))
    names = ['ffn_w_in', 'ffn_dw_w', 'ffn_dw_b', 'ffn_w_out']
    y, st_new = pl.pallas_call(
        functools.partial(_conv_ffn_kernel, Tt, Bb, F),
        grid=(B // Bb, T // Tt),
        in_specs=[row, stspec, _lspec(W['norm_gain'], i)] + [_lspec(W[n], i) for n in names],
        out_specs=[row, stspec],
        out_shape=[jax.ShapeDtypeStruct((T, B, D), _F32), jax.ShapeDtypeStruct((W1, B, F2), _F32)],
        scratch_shapes=[pltpu.VMEM((W1 + Tt, Bb, F2), _F32)],
        compiler_params=_params(1),
        name='conv_ffn',
    )(x, st, W['norm_gain'], *[W[n] for n in names])
    return y, st_new


_MATRICES = ('rwkv_wr', 'rwkv_wk', 'rwkv_wv', 'rwkv_wo', 'rwkv_w1', 'rwkv_w2', 'rwkv_a1',
             'rwkv_a2', 'rwkv_v1', 'rwkv_v2', 'rwkv_g1', 'rwkv_g2', 'conv_w_in', 'conv_w_out',
             'ffn_w_in', 'ffn_w_out')
_VECTORS = ('rwkv_w0', 'rwkv_a0', 'rwkv_v0', 'conv_b_in', 'conv_dw_b', 'conv_ln_w', 'conv_ln_b',
            'conv_b_out', 'ffn_dw_b')


def _prep_weights(p):
    W = dict(p)
    for n in _MATRICES:
        W[n] = p[n].astype(_BF)
    for n in _VECTORS:
        W[n] = p[n].reshape(p[n].shape[0], 1, p[n].shape[1])
    return W


def _trunk(x, st_wkv, st_shift, st_glu, st_ffn, W):
    x = x.swapaxes(0, 1)
    v_first = None
    wkv_l, shift_l, glu_l, ffn_l = [], [], [], []
    for i in range(DEPTH):
        j = i // 2
        if i % 2 == 0:
            r, w, k, v, a, gate, sh = _rwkv_proj(x, st_shift[j], v_first, W, i, j)
            if v_first is None:
                v_first = v
            o, s_new = _wkv(r, w, k, v, a, st_wkv[j], W, j)
            x = _rwkv_out(x, o, gate, W, i, j)
            wkv_l.append(s_new)
            shift_l.append(sh)
        else:
            x, gb = _conformer(x, st_glu[j].swapaxes(0, 1), W, i, j)
            glu_l.append(gb.swapaxes(0, 1))
        x, fb = _conv_ffn(x, st_ffn[i].swapaxes(0, 1), W, i)
        ffn_l.append(fb.swapaxes(0, 1))
    return (x.swapaxes(0, 1), jnp.stack(wkv_l), jnp.stack(shift_l), jnp.stack(glu_l),
            jnp.stack(ffn_l))


def kernel(x_prompt, x_sample, state_rwkv_wkv, state_rwkv_shift, state_conv_glu, state_ffn_conv, norm_gain, rwkv_mix, rwkv_wr, rwkv_wk, rwkv_wv, rwkv_wo, rwkv_w0, rwkv_w1, rwkv_w2, rwkv_a0, rwkv_a1, rwkv_a2, rwkv_v0, rwkv_v1, rwkv_v2, rwkv_g1, rwkv_g2, rwkv_k_k, rwkv_k_a, rwkv_r_k, rwkv_lnx_w, rwkv_lnx_b, conv_w_in, conv_b_in, conv_dw_w, conv_dw_b, conv_ln_w, conv_ln_b, conv_w_out, conv_b_out, ffn_w_in, ffn_dw_w, ffn_dw_b, ffn_w_out):
    p = dict(norm_gain=norm_gain, rwkv_mix=rwkv_mix, rwkv_wr=rwkv_wr, rwkv_wk=rwkv_wk,
             rwkv_wv=rwkv_wv, rwkv_wo=rwkv_wo, rwkv_w0=rwkv_w0, rwkv_w1=rwkv_w1,
             rwkv_w2=rwkv_w2, rwkv_a0=rwkv_a0, rwkv_a1=rwkv_a1, rwkv_a2=rwkv_a2,
             rwkv_v0=rwkv_v0, rwkv_v1=rwkv_v1, rwkv_v2=rwkv_v2, rwkv_g1=rwkv_g1,
             rwkv_g2=rwkv_g2, rwkv_k_k=rwkv_k_k, rwkv_k_a=rwkv_k_a, rwkv_r_k=rwkv_r_k,
             rwkv_lnx_w=rwkv_lnx_w, rwkv_lnx_b=rwkv_lnx_b, conv_w_in=conv_w_in,
             conv_b_in=conv_b_in, conv_dw_w=conv_dw_w, conv_dw_b=conv_dw_b,
             conv_ln_w=conv_ln_w, conv_ln_b=conv_ln_b, conv_w_out=conv_w_out,
             conv_b_out=conv_b_out, ffn_w_in=ffn_w_in, ffn_dw_w=ffn_dw_w,
             ffn_dw_b=ffn_dw_b, ffn_w_out=ffn_w_out)
    W = _prep_weights(p)
    bp = x_prompt.shape[0]
    n_rwkv, n_conv, depth = state_rwkv_wkv.shape[0], state_conv_glu.shape[0], state_ffn_conv.shape[0]
    z_wkv = jnp.zeros((n_rwkv, bp) + state_rwkv_wkv.shape[2:], _F32)
    z_shift = jnp.zeros((n_rwkv, bp, D_MODEL), _F32)
    z_glu = jnp.zeros((n_conv, bp) + state_conv_glu.shape[2:], _F32)
    z_ffn = jnp.zeros((depth, bp) + state_ffn_conv.shape[2:], _F32)
    yp, p_wkv, p_shift, p_glu, p_ffn = _trunk(x_prompt, z_wkv, z_shift, z_glu, z_ffn, W)
    ys, s_wkv, s_shift, s_glu, s_ffn = _trunk(x_sample, state_rwkv_wkv, state_rwkv_shift,
                                              state_conv_glu, state_ffn_conv, W)
    wdt = state_rwkv_wkv.dtype
    return (yp, ys, p_wkv.astype(wdt), p_shift, p_glu, p_ffn,
            s_wkv.astype(wdt), s_shift, s_glu, s_ffn)
```

```python
import functools

import jax
import jax.numpy as jnp
from jax import lax
from jax.experimental import pallas as pl
from jax.experimental.pallas import tpu as pltpu

D_MODEL = 1024
HEAD_SIZE = 64
N_HEADS = D_MODEL // HEAD_SIZE
DEPTH = 4
CONV_WIDTH = 31
FFN_CONV_WIDTH = 3
NORM_EPS = 1e-6
LN_EPS = 1e-5
LNX_EPS = 64e-5

LANES = 128
SUBLANES = 8
ROWS = 256
FFN_CHUNK = 256
WKV_T_CHUNK = 32
VMEM_LIMIT = 56 * 1024 * 1024

_BF = jnp.bfloat16
_F32 = jnp.float32


def _tiles(T, B):
    Tt = min(T, ROWS // SUBLANES)
    Bb = min(B, ROWS // Tt)
    assert T % Tt == 0 and B % Bb == 0 and Bb % SUBLANES == 0, (T, B)
    return Tt, Bb


def _lspec(arr, layer):
    nd = arr.ndim - 1
    return pl.BlockSpec((None,) + arr.shape[1:], lambda *_: (layer,) + (0,) * nd,
                        pipeline_mode=pl.Buffered(1))


def _rms(x, g):
    return x * lax.rsqrt(jnp.mean(x * x, axis=-1, keepdims=True) + NORM_EPS) * g


def _bdot(a, w):
    return jnp.dot(a.astype(_BF), w, preferred_element_type=_F32)


def _sigmoid(x):
    return 1.0 / (1.0 + jnp.exp(-x))


def _params(n_par):
    return pltpu.CompilerParams(
        dimension_semantics=("parallel",) * n_par + ("arbitrary",),
        vmem_limit_bytes=VMEM_LIMIT)


def _row_spec(Tt, Bb):
    return pl.BlockSpec((Tt, Bb, D_MODEL), lambda b, t: (t, b, 0))


def _head_halves(a, b, low):
    return (jnp.where(low, a, pltpu.roll(b, HEAD_SIZE, axis=1)),
            jnp.where(low, pltpu.roll(a, HEAD_SIZE, axis=1), b))


def _to_lanes(tile, dst_ref, Tc, lane0):
    for q in range(Tc // 2):
        _to_lanes_pair(tile, dst_ref, q, lane0)


def _to_lanes_pair(tile, dst_ref, q, lane0):
    low = lax.broadcasted_iota(jnp.int32, (SUBLANES, LANES), 1) < HEAD_SIZE
    ls = slice(lane0, lane0 + LANES)
    rows = []
    for p in range(D_MODEL // LANES):
        cs = slice(p * LANES, (p + 1) * LANES)
        rows.extend(_head_halves(tile(2 * q, cs), tile(2 * q + 1, cs), low))
    z = jnp.concatenate(rows, axis=0).T
    dst_ref[2 * q, :, ls] = z[:HEAD_SIZE]
    dst_ref[2 * q + 1, :, ls] = z[HEAD_SIZE:]


def _from_lanes(src_ref, dst_ref, Tc):
    low = lax.broadcasted_iota(jnp.int32, (SUBLANES, LANES), 1) < HEAD_SIZE
    for q in range(Tc // 2):
        z = jnp.concatenate([src_ref[2 * q], src_ref[2 * q + 1]], axis=0).T
        for p in range(D_MODEL // LANES):
            cs = slice(p * LANES, (p + 1) * LANES)
            even = z[(2 * p) * SUBLANES:(2 * p + 1) * SUBLANES]
            odd = z[(2 * p + 1) * SUBLANES:(2 * p + 2) * SUBLANES]
            dst_ref[2 * q, :, cs] = jnp.where(low, even, pltpu.roll(odd, HEAD_SIZE, axis=1))
            dst_ref[2 * q + 1, :, cs] = jnp.where(low, pltpu.roll(even, HEAD_SIZE, axis=1), odd)


PROJ_SEQS = 5


def _rwkv_proj_kernel(has_vres, Tt, Bb, *refs):
    refs = list(refs)
    x_ref, shift_ref = refs[:2]
    refs = refs[2:]
    if has_vres:
        vfirst_ref = refs.pop(0)
    (gain_ref, mix_ref, wr_ref, wk_ref, wv_ref, w0_ref, w1_ref, w2_ref,
     a0_ref, a1_ref, a2_ref, g1_ref, g2_ref) = refs[:13]
    refs = refs[13:]
    if has_vres:
        v0_ref, v1_ref, v2_ref = refs[:3]
        refs = refs[3:]
    lane_outs = refs[:PROJ_SEQS]
    refs = refs[PROJ_SEQS:]
    g_out = refs.pop(0)
    if not has_vres:
        vn_out = refs.pop(0)
    shift_out, xs = refs[:2]
    cur = refs[2:2 + PROJ_SEQS + 1]
    nxt = refs[2 + PROJ_SEQS + 1:]
    V_IDX = 3

    R = Tt * Bb
    s = pl.program_id(1)

    @pl.when(s == 0)
    def _():
        xs[0] = shift_ref[...]
        for c in cur:
            c[...] = jnp.zeros((Tt, Bb, D_MODEL), _F32)

    g_out[...] = cur[PROJ_SEQS][...]
    if not has_vres:
        vn_out[...] = cur[V_IDX][...]
    pieces = [(c, out_ref, gi, q)
              for c, out_ref in zip(cur[:PROJ_SEQS], lane_outs)
              for gi in range(Bb // SUBLANES) for q in range(Tt // 2)]
    n_slots = 10 if has_vres else 9
    per_slot = -(-len(pieces) // n_slots)

    def write_out_some():
        for _ in range(min(per_slot, len(pieces))):
            c, out_ref, gi, q = pieces.pop(0)
            bs = slice(gi * SUBLANES, (gi + 1) * SUBLANES)
            _to_lanes_pair(lambda t, cs: c[t, bs, cs], out_ref, q, gi * LANES)

    h = _rms(x_ref[...].reshape(R, D_MODEL), gain_ref[0:1, :])
    h3 = h.reshape(Tt, Bb, D_MODEL)
    xs[1:1 + Tt] = h3
    xx = (xs[0:Tt] - h3).reshape(R, D_MODEL)
    last = h3[Tt - 1]
    xs[0] = last
    shift_out[...] = last

    def mixed(m):
        return h + xx * mix_ref[m:m + 1, :]

    def out3(v):
        return v.reshape(Tt, Bb, D_MODEL)

    xv = mixed(3)
    lw = _bdot(mixed(1), w1_ref[...])
    write_out_some()
    la = _bdot(mixed(4), a1_ref[...])
    write_out_some()
    lg = _bdot(mixed(5), g1_ref[...])
    write_out_some()
    if has_vres:
        lv = _bdot(xv, v1_ref[...])
        write_out_some()
    nxt[0][...] = out3(_bdot(mixed(0), wr_ref[...]))
    write_out_some()
    nxt[2][...] = out3(_bdot(mixed(2), wk_ref[...]))
    write_out_some()
    v = _bdot(xv, wv_ref[...])
    write_out_some()
    wl = w0_ref[...] + _bdot(jnp.tanh(lw), w2_ref[...])
    w_log = -(jnp.maximum(-wl, 0.0) + jnp.log(1.0 + jnp.exp(-jnp.abs(wl)))) - 0.5
    nxt[1][...] = out3(jnp.exp(-jnp.exp(w_log)))
    write_out_some()
    nxt[4][...] = out3(_sigmoid(a0_ref[...] + _bdot(la, a2_ref[...])))
    write_out_some()
    nxt[PROJ_SEQS][...] = out3(_bdot(_sigmoid(lg), g2_ref[...]))
    write_out_some()
    if has_vres:
        vf = vfirst_ref[...].reshape(R, D_MODEL)
        v = v + (vf - v) * _sigmoid(v0_ref[...] + _bdot(lv, v2_ref[...]))
    nxt[V_IDX][...] = out3(v)
    assert not pieces

    for c, n in zip(cur, nxt):
        c[...] = n[...]


def _rwkv_proj(x, shift, v_first, W, i, j):
    T, B, D = x.shape
    Tt, Bb = _tiles(T, B)
    nT = T // Tt
    has_vres = v_first is not None
    row_in = pl.BlockSpec((Tt, Bb, D), lambda b, s: (jnp.minimum(s, nT - 1), b, 0))
    row_out = pl.BlockSpec((Tt, Bb, D), lambda b, s: (jnp.maximum(s - 1, 0), b, 0))
    st = pl.BlockSpec((Bb, D), lambda b, s: (b, 0))
    args = [x, shift]
    specs = [row_in, st]
    if has_vres:
        args.append(v_first)
        specs.append(row_in)
    names = ['rwkv_mix', 'rwkv_wr', 'rwkv_wk', 'rwkv_wv', 'rwkv_w0', 'rwkv_w1', 'rwkv_w2',
             'rwkv_a0', 'rwkv_a1', 'rwkv_a2', 'rwkv_g1', 'rwkv_g2']
    args.append(W['norm_gain'])
    specs.append(_lspec(W['norm_gain'], i))
    for n in names:
        args.append(W[n])
        specs.append(_lspec(W[n], j))
    if has_vres:
        for n in ('rwkv_v0', 'rwkv_v1', 'rwkv_v2'):
            args.append(W[n])
            specs.append(_lspec(W[n], j - 1))
    act = jax.ShapeDtypeStruct((T, B, D), _F32)
    n_lanes = Bb // SUBLANES * LANES
    lane_spec = pl.BlockSpec((Tt, HEAD_SIZE, n_lanes), lambda b, s: (jnp.maximum(s - 1, 0), 0, b))
    lane_act = jax.ShapeDtypeStruct((T, HEAD_SIZE, B // SUBLANES * LANES), _F32)
    n_nat = 1 if has_vres else 2
    outs = pl.pallas_call(
        functools.partial(_rwkv_proj_kernel, has_vres, Tt, Bb),
        grid=(B // Bb, nT + 1),
        in_specs=specs,
        out_specs=[lane_spec] * PROJ_SEQS + [row_out] * n_nat + [st],
        out_shape=[lane_act] * PROJ_SEQS + [act] * n_nat + [jax.ShapeDtypeStruct((B, D), _F32)],
        scratch_shapes=([pltpu.VMEM((1 + Tt, Bb, D), _F32)]
                        + [pltpu.VMEM((Tt, Bb, D), _F32)] * (2 * (PROJ_SEQS + 1))),
        compiler_params=_params(1),
        name='rwkv_proj_vres' if has_vres else 'rwkv_proj',
    )(*args)
    return outs


def _wkv_kernel(Tc, r_ref, w_ref, k_ref, v_ref, a_ref, kk_ref, ka_ref, rk_ref, lnw_ref,
                lnb_ref, s0_ref, o_ref, s_ref, an_s, bn_s, km_s, o_s):
    N = HEAD_SIZE

    @pl.when(pl.program_id(1) == 0)
    def _():
        s_ref[...] = s0_ref[...]

    k = k_ref[...]
    ag = a_ref[...]
    kk = k * kk_ref[...]
    kk = kk / jnp.maximum(jnp.sqrt(jnp.sum(kk * kk, axis=1, keepdims=True)), 1e-12)
    km = k * (1.0 + (ag - 1.0) * ka_ref[...])
    an_s[...] = -kk
    bn_s[...] = kk * ag
    km_s[...] = km

    def row(ref, t, j):
        return ref[t, pl.ds(j, 1), :]

    sa = jnp.zeros((N, LANES), _F32)
    for j in range(N):
        sa = sa + s_ref[j] * row(an_s, 0, j)

    def step(t, sa):
        tn = jnp.minimum(t + 1, Tc - 1)
        v = v_ref[t]
        y = jnp.zeros((N, LANES), _F32)
        sa_next = jnp.zeros((N, LANES), _F32)
        for j in range(N):
            sn = s_ref[j] * row(w_ref, t, j) + sa * row(bn_s, t, j) + v * row(km_s, t, j)
            s_ref[j] = sn
            y = y + sn * row(r_ref, t, j)
            sa_next = sa_next + sn * row(an_s, tn, j)
        o_s[t] = y
        return sa_next

    lax.fori_loop(0, Tc, step, sa)

    y = o_s[...]
    mu = jnp.mean(y, axis=1, keepdims=True)
    d = y - mu
    var = jnp.mean(d * d, axis=1, keepdims=True)
    yn = d * lax.rsqrt(var + LNX_EPS) * lnw_ref[...] + lnb_ref[...]
    bonus = jnp.sum(r_ref[...] * km * rk_ref[...], axis=1, keepdims=True) * v_ref[...]
    o_s[...] = yn + bonus
    _from_lanes(o_s, o_ref, Tc)


def _lane_param(p):
    return jnp.repeat(p.reshape(N_HEADS, HEAD_SIZE).T, LANES // N_HEADS, axis=1)


def _wkv(r, w, k, v, a, s0, W, j):
    T, N, L = r.shape
    G = L // LANES
    B = G * SUBLANES
    assert N == HEAD_SIZE and N_HEADS * SUBLANES == LANES
    Tc = min(T, WKV_T_CHUNK)
    assert T % Tc == 0 and Tc % 2 == 0
    s0t = (s0.astype(_F32).reshape(G, SUBLANES, N_HEADS, N, N).transpose(4, 3, 0, 2, 1)
           .reshape(N, N, L))
    pars = [_lane_param(W[n][j].reshape(-1))
            for n in ('rwkv_k_k', 'rwkv_k_a', 'rwkv_r_k', 'rwkv_lnx_w', 'rwkv_lnx_b')]
    seq_spec = pl.BlockSpec((Tc, N, LANES), lambda g, t: (t, 0, g))
    out_spec = pl.BlockSpec((Tc, SUBLANES, D_MODEL), lambda g, t: (t, g, 0))
    st_spec = pl.BlockSpec((N, N, LANES), lambda g, t: (0, 0, g))
    par_spec = pl.BlockSpec((N, LANES), lambda g, t: (0, 0))
    o, sT = pl.pallas_call(
        functools.partial(_wkv_kernel, Tc),
        grid=(G, T // Tc),
        in_specs=[seq_spec] * 5 + [par_spec] * 5 + [st_spec],
        out_specs=[out_spec, st_spec],
        out_shape=[jax.ShapeDtypeStruct((T, B, D_MODEL), _F32),
                   jax.ShapeDtypeStruct((N, N, L), _F32)],
        scratch_shapes=[pltpu.VMEM((Tc, N, LANES), _F32)] * 4,
        compiler_params=_params(1),
        name='wkv_scan',
    )(r, w, k, v, a, *pars, s0t)
    s_new = (sT.reshape(N, N, G, N_HEADS, SUBLANES).transpose(2, 4, 3, 1, 0)
             .reshape(B, N_HEADS, N, N))
    return o, s_new


def _rwkv_out_kernel(Tt, Bb, x_ref, o_ref, g_ref, wo_ref, gain_ref, y_ref):
    R = Tt * Bb
    m = _bdot((o_ref[...] * g_ref[...]).reshape(R, D_MODEL), wo_ref[...])
    y_ref[...] = x_ref[...] + _rms(m, gain_ref[1:2, :]).reshape(Tt, Bb, D_MODEL)


def _rwkv_out(x, o, g, W, i, j):
    T, B, D = x.shape
    Tt, Bb = _tiles(T, B)
    row = _row_spec(Tt, Bb)
    return pl.pallas_call(
        functools.partial(_rwkv_out_kernel, Tt, Bb),
        grid=(B // Bb, T // Tt),
        in_specs=[row, row, row, _lspec(W['rwkv_wo'], j), _lspec(W['norm_gain'], i)],
        out_specs=row,
        out_shape=jax.ShapeDtypeStruct((T, B, D), _F32),
        compiler_params=_params(1),
        name='rwkv_out',
    )(x, o, g, W['rwkv_wo'], W['norm_gain'])


def _conformer_kernel(Tt, Bb, x_ref, st_ref, gain_ref, win_ref, bin_ref, dww_ref, dwb_ref,
                      lnw_ref, lnb_ref, wout_ref, bout_ref, y_ref, st_out, buf, cbuf):
    R = Tt * Bb
    D = D_MODEL
    W1 = CONV_WIDTH - 1
    t = pl.program_id(1)
    x = x_ref[...].reshape(R, D)
    hh = _bdot(_rms(x, gain_ref[0:1, :]), win_ref[...]) + bin_ref[...]
    glu = hh[:, :D] * _sigmoid(hh[:, D:])

    @pl.when(t == 0)
    def _():
        buf[0:W1] = st_ref[...]

    buf[W1:W1 + Tt] = glu.reshape(Tt, Bb, D)
    for c in range(D // LANES):
        cs = slice(c * LANES, (c + 1) * LANES)
        acc = jnp.zeros((Tt, Bb, LANES), _F32) + dwb_ref[:, cs]
        for kk in range(CONV_WIDTH):
            acc = acc + buf[kk:kk + Tt, :, cs] * dww_ref[kk:kk + 1, cs]
        cbuf[:, :, cs] = acc
    new_hist = buf[Tt:Tt + W1]
    st_out[...] = new_hist
    buf[0:W1] = new_hist

    cv = cbuf[...].reshape(R, D)
    mu = jnp.mean(cv, axis=-1, keepdims=True)
    d = cv - mu
    var = jnp.mean(d * d, axis=-1, keepdims=True)
    ln = d * lax.rsqrt(var + LN_EPS) * lnw_ref[...] + lnb_ref[...]
    m = _bdot(ln * _sigmoid(ln), wout_ref[...]) + bout_ref[...]
    y_ref[...] = (x + _rms(m, gain_ref[1:2, :])).reshape(Tt, Bb, D)


def _conformer(x, st, W, i, j):
    T, B, D = x.shape
    Tt, Bb = _tiles(T, B)
    W1 = CONV_WIDTH - 1
    row = _row_spec(Tt, Bb)
    stspec = pl.BlockSpec((W1, Bb, D), lambda b, t: (0, b, 0))
    names = ['conv_w_in', 'conv_b_in', 'conv_dw_w', 'conv_dw_b', 'conv_ln_w', 'conv_ln_b',
             'conv_w_out', 'conv_b_out']
    y, st_new = pl.pallas_call(
        functools.partial(_conformer_kernel, Tt, Bb),
        grid=(B // Bb, T // Tt),
        in_specs=[row, stspec, _lspec(W['norm_gain'], i)] + [_lspec(W[n], j) for n in names],
        out_specs=[row, stspec],
        out_shape=[jax.ShapeDtypeStruct((T, B, D), _F32), jax.ShapeDtypeStruct((W1, B, D), _F32)],
        scratch_shapes=[pltpu.VMEM((W1 + Tt, Bb, D), _F32), pltpu.VMEM((Tt, Bb, D), _F32)],
        compiler_params=_params(1),
        name='conformer_conv',
    )(x, st, W['norm_gain'], *[W[n] for n in names])
    return y, st_new


def _conv_ffn_kernel(Tt, Bb, F, x_ref, st_ref, gain_ref, win_ref, dww_ref, dwb_ref, wout_ref,
                     y_ref, st_out, buf):
    R = Tt * Bb
    D = D_MODEL
    W1 = FFN_CONV_WIDTH - 1
    C = FFN_CHUNK
    t = pl.program_id(1)
    x = x_ref[...].reshape(R, D)
    hb = _rms(x, gain_ref[2:3, :]).astype(_BF)

    @pl.when(t == 0)
    def _():
        buf[0:W1] = st_ref[...]

    def up(c):
        for col0 in (c * C, F + c * C):
            cs = slice(col0, col0 + C)
            u = jnp.dot(hb, win_ref[:, cs], preferred_element_type=_F32)
            buf[W1:W1 + Tt, :, cs] = u.reshape(Tt, Bb, C)

    def conv(col0):
        cs = slice(col0, col0 + C)
        acc = dwb_ref[:, cs] + buf[W1:W1 + Tt, :, cs] * dww_ref[W1:W1 + 1, cs]
        for kk in range(W1):
            acc = acc + buf[kk:kk + Tt, :, cs] * dww_ref[kk:kk + 1, cs]
        return acc.reshape(R, C)

    n_chunks = F // C
    acc = jnp.zeros((R, D), _F32)
    up(0)
    for c in range(n_chunks):
        if c + 1 < n_chunks:
            up(c + 1)
        gc = conv(c * C)
        uc = conv(F + c * C)
        z = gc * _sigmoid(gc) * uc
        acc = acc + _bdot(z, wout_ref[c * C:(c + 1) * C, :])
    new_hist = buf[Tt:Tt + W1]
    st_out[...] = new_hist
    buf[0:W1] = new_hist
    y_ref[...] = (x + _rms(acc, gain_ref[3:4, :])).reshape(Tt, Bb, D)


def _conv_ffn(x, st, W, i):
    T, B, D = x.shape
    Tt, Bb = _tiles(T, B)
    F2 = W['ffn_w_in'].shape[-1]
    F = F2 // 2
    assert F % FFN_CHUNK == 0
    W1 = FFN_CONV_WIDTH - 1
    row = _row_spec(Tt, Bb)
    stspec = pl.BlockSpec((W1, Bb, F2), lambda b, t: (0, b, 0))
    names = ['ffn_w_in', 'ffn_dw_w', 'ffn_dw_b', 'ffn_w_out']
    y, st_new = pl.pallas_call(
        functools.partial(_conv_ffn_kernel, Tt, Bb, F),
        grid=(B // Bb, T // Tt),
        in_specs=[row, stspec, _lspec(W['norm_gain'], i)] + [_lspec(W[n], i) for n in names],
        out_specs=[row, stspec],
        out_shape=[jax.ShapeDtypeStruct((T, B, D), _F32), jax.ShapeDtypeStruct((W1, B, F2), _F32)],
        scratch_shapes=[pltpu.VMEM((W1 + Tt, Bb, F2), _F32)],
        compiler_params=_params(1),
        name='conv_ffn',
    )(x, st, W['norm_gain'], *[W[n] for n in names])
    return y, st_new


_MATRICES = ('rwkv_wr', 'rwkv_wk', 'rwkv_wv', 'rwkv_wo', 'rwkv_w1', 'rwkv_w2', 'rwkv_a1',
             'rwkv_a2', 'rwkv_v1', 'rwkv_v2', 'rwkv_g1', 'rwkv_g2', 'conv_w_in', 'conv_w_out',
             'ffn_w_in', 'ffn_w_out')
_VECTORS = ('rwkv_w0', 'rwkv_a0', 'rwkv_v0', 'conv_b_in', 'conv_dw_b', 'conv_ln_w', 'conv_ln_b',
            'conv_b_out', 'ffn_dw_b')


def _prep_weights(p):
    W = dict(p)
    for n in _MATRICES:
        W[n] = p[n].astype(_BF)
    for n in _VECTORS:
        W[n] = p[n].reshape(p[n].shape[0], 1, p[n].shape[1])
    return W


def _trunk(x, st_wkv, st_shift, st_glu, st_ffn, W):
    x = x.swapaxes(0, 1)
    v_first = None
    wkv_l, shift_l, glu_l, ffn_l = [], [], [], []
    for i in range(DEPTH):
        j = i // 2
        if i % 2 == 0:
            outs = _rwkv_proj(x, st_shift[j], v_first, W, i, j)
            r, w, k, v, a, gate = outs[:6]
            sh = outs[-1]
            if v_first is None:
                v_first = outs[6]
            o, s_new = _wkv(r, w, k, v, a, st_wkv[j], W, j)
            x = _rwkv_out(x, o, gate, W, i, j)
            wkv_l.append(s_new)
            shift_l.append(sh)
        else:
            x, gb = _conformer(x, st_glu[j].swapaxes(0, 1), W, i, j)
            glu_l.append(gb.swapaxes(0, 1))
        x, fb = _conv_ffn(x, st_ffn[i].swapaxes(0, 1), W, i)
        ffn_l.append(fb.swapaxes(0, 1))
    return (x.swapaxes(0, 1), jnp.stack(wkv_l), jnp.stack(shift_l), jnp.stack(glu_l),
            jnp.stack(ffn_l))


def kernel(x_prompt, x_sample, state_rwkv_wkv, state_rwkv_shift, state_conv_glu, state_ffn_conv, norm_gain, rwkv_mix, rwkv_wr, rwkv_wk, rwkv_wv, rwkv_wo, rwkv_w0, rwkv_w1, rwkv_w2, rwkv_a0, rwkv_a1, rwkv_a2, rwkv_v0, rwkv_v1, rwkv_v2, rwkv_g1, rwkv_g2, rwkv_k_k, rwkv_k_a, rwkv_r_k, rwkv_lnx_w, rwkv_lnx_b, conv_w_in, conv_b_in, conv_dw_w, conv_dw_b, conv_ln_w, conv_ln_b, conv_w_out, conv_b_out, ffn_w_in, ffn_dw_w, ffn_dw_b, ffn_w_out):
    p = dict(norm_gain=norm_gain, rwkv_mix=rwkv_mix, rwkv_wr=rwkv_wr, rwkv_wk=rwkv_wk,
             rwkv_wv=rwkv_wv, rwkv_wo=rwkv_wo, rwkv_w0=rwkv_w0, rwkv_w1=rwkv_w1,
             rwkv_w2=rwkv_w2, rwkv_a0=rwkv_a0, rwkv_a1=rwkv_a1, rwkv_a2=rwkv_a2,
             rwkv_v0=rwkv_v0, rwkv_v1=rwkv_v1, rwkv_v2=rwkv_v2, rwkv_g1=rwkv_g1,
             rwkv_g2=rwkv_g2, rwkv_k_k=rwkv_k_k, rwkv_k_a=rwkv_k_a, rwkv_r_k=rwkv_r_k,
             rwkv_lnx_w=rwkv_lnx_w, rwkv_lnx_b=rwkv_lnx_b, conv_w_in=conv_w_in,
             conv_b_in=conv_b_in, conv_dw_w=conv_dw_w, conv_dw_b=conv_dw_b,
             conv_ln_w=conv_ln_w, conv_ln_b=conv_ln_b, conv_w_out=conv_w_out,
             conv_b_out=conv_b_out, ffn_w_in=ffn_w_in, ffn_dw_w=ffn_dw_w,
             ffn_dw_b=ffn_dw_b, ffn_w_out=ffn_w_out)
    W = _prep_weights(p)
    bp = x_prompt.shape[0]
    n_rwkv, n_conv, depth = state_rwkv_wkv.shape[0], state_conv_glu.shape[0], state_ffn_conv.shape[0]
    z_wkv = jnp.zeros((n_rwkv, bp) + state_rwkv_wkv.shape[2:], _F32)
    z_shift = jnp.zeros((n_rwkv, bp, D_MODEL), _F32)
    z_glu = jnp.zeros((n_conv, bp) + state_conv_glu.shape[2:], _F32)
    z_ffn = jnp.zeros((depth, bp) + state_ffn_conv.shape[2:], _F32)
    yp, p_wkv, p_shift, p_glu, p_ffn = _trunk(x_prompt, z_wkv, z_shift, z_glu, z_ffn, W)
    ys, s_wkv, s_shift, s_glu, s_ffn = _trunk(x_sample, state_rwkv_wkv, state_rwkv_shift,
                                              state_conv_glu, state_ffn_conv, W)
    wdt = state_rwkv_wkv.dtype
    return (yp, ys, p_wkv.astype(wdt), p_shift, p_glu, p_ffn,
            s_wkv.astype(wdt), s_shift, s_glu, s_ffn)
```

```python
import functools

import jax
import jax.numpy as jnp
from jax import lax
from jax.experimental import pallas as pl
from jax.experimental.pallas import tpu as pltpu

D_MODEL = 1024
HEAD_SIZE = 64
N_HEADS = D_MODEL // HEAD_SIZE
DEPTH = 4
CONV_WIDTH = 31
FFN_CONV_WIDTH = 3
NORM_EPS = 1e-6
LN_EPS = 1e-5
LNX_EPS = 64e-5

LANES = 128
SUBLANES = 8
ROWS = 256
FFN_CHUNK = 256
GLU_COLS = 256
WKV_T_CHUNK = 32
VMEM_LIMIT = 56 * 1024 * 1024

_BF = jnp.bfloat16
_F32 = jnp.float32


def _tiles(T, B):
    Tt = min(T, ROWS // SUBLANES)
    Bb = min(B, ROWS // Tt)
    assert T % Tt == 0 and B % Bb == 0 and Bb % SUBLANES == 0, (T, B)
    return Tt, Bb


def _lspec(arr, layer):
    nd = arr.ndim - 1
    return pl.BlockSpec((None,) + arr.shape[1:], lambda *_: (layer,) + (0,) * nd,
                        pipeline_mode=pl.Buffered(1))


def _rms(x, g):
    return x * lax.rsqrt(jnp.mean(x * x, axis=-1, keepdims=True) + NORM_EPS) * g


def _bdot(a, w):
    return jnp.dot(a.astype(_BF), w, preferred_element_type=_F32)


def _sigmoid(x):
    return 1.0 / (1.0 + jnp.exp(-x))


def _params(n_par):
    return pltpu.CompilerParams(
        dimension_semantics=("parallel",) * n_par + ("arbitrary",),
        vmem_limit_bytes=VMEM_LIMIT)


def _row_spec(Tt, Bb):
    return pl.BlockSpec((Tt, Bb, D_MODEL), lambda b, t: (t, b, 0))


def _head_halves(a, b, low):
    return (jnp.where(low, a, pltpu.roll(b, HEAD_SIZE, axis=1)),
            jnp.where(low, pltpu.roll(a, HEAD_SIZE, axis=1), b))


def _to_lanes(tile, dst_ref, Tc, lane0):
    for q in range(Tc // 2):
        _to_lanes_pair(tile, dst_ref, q, lane0)


def _to_lanes_pair(tile, dst_ref, q, lane0):
    low = lax.broadcasted_iota(jnp.int32, (SUBLANES, LANES), 1) < HEAD_SIZE
    ls = slice(lane0, lane0 + LANES)
    rows = []
    for p in range(D_MODEL // LANES):
        cs = slice(p * LANES, (p + 1) * LANES)
        rows.extend(_head_halves(tile(2 * q, cs), tile(2 * q + 1, cs), low))
    z = jnp.concatenate(rows, axis=0).T
    dst_ref[2 * q, :, ls] = z[:HEAD_SIZE]
    dst_ref[2 * q + 1, :, ls] = z[HEAD_SIZE:]


def _from_lanes(src_ref, dst_ref, Tc, lane0, seq0):
    low = lax.broadcasted_iota(jnp.int32, (SUBLANES, LANES), 1) < HEAD_SIZE
    ls = slice(lane0, lane0 + LANES)
    bs = slice(seq0, seq0 + SUBLANES)
    for q in range(Tc // 2):
        z = jnp.concatenate([src_ref[2 * q, :, ls], src_ref[2 * q + 1, :, ls]], axis=0).T
        for p in range(D_MODEL // LANES):
            cs = slice(p * LANES, (p + 1) * LANES)
            even = z[(2 * p) * SUBLANES:(2 * p + 1) * SUBLANES]
            odd = z[(2 * p + 1) * SUBLANES:(2 * p + 2) * SUBLANES]
            dst_ref[2 * q, bs, cs] = jnp.where(low, even, pltpu.roll(odd, HEAD_SIZE, axis=1))
            dst_ref[2 * q + 1, bs, cs] = jnp.where(low, pltpu.roll(even, HEAD_SIZE, axis=1), odd)


PROJ_SEQS = 5


def _rwkv_proj_kernel(has_vres, Tt, Bb, *refs):
    refs = list(refs)
    x_ref, shift_ref = refs[:2]
    refs = refs[2:]
    if has_vres:
        vfirst_ref = refs.pop(0)
    (gain_ref, mix_ref, wr_ref, wk_ref, wv_ref, w0_ref, w1_ref, w2_ref,
     a0_ref, a1_ref, a2_ref, g1_ref, g2_ref) = refs[:13]
    refs = refs[13:]
    if has_vres:
        v0_ref, v1_ref, v2_ref = refs[:3]
        refs = refs[3:]
    lane_outs = refs[:PROJ_SEQS]
    refs = refs[PROJ_SEQS:]
    g_out = refs.pop(0)
    if not has_vres:
        vn_out = refs.pop(0)
    shift_out, xs = refs[:2]
    cur = refs[2:2 + PROJ_SEQS + 1]
    nxt = refs[2 + PROJ_SEQS + 1:]
    V_IDX = 3

    R = Tt * Bb
    s = pl.program_id(1)

    @pl.when(s == 0)
    def _():
        xs[0] = shift_ref[...]
        for c in cur:
            c[...] = jnp.zeros((Tt, Bb, D_MODEL), _F32)

    g_out[...] = cur[PROJ_SEQS][...]
    if not has_vres:
        vn_out[...] = cur[V_IDX][...]
    pieces = [(c, out_ref, gi, q)
              for c, out_ref in zip(cur[:PROJ_SEQS], lane_outs)
              for gi in range(Bb // SUBLANES) for q in range(Tt // 2)]
    n_slots = 10 if has_vres else 9
    per_slot = -(-len(pieces) // n_slots)

    def write_out_some():
        for _ in range(min(per_slot, len(pieces))):
            c, out_ref, gi, q = pieces.pop(0)
            bs = slice(gi * SUBLANES, (gi + 1) * SUBLANES)
            _to_lanes_pair(lambda t, cs: c[t, bs, cs], out_ref, q, gi * LANES)

    h = _rms(x_ref[...].reshape(R, D_MODEL), gain_ref[0:1, :])
    h3 = h.reshape(Tt, Bb, D_MODEL)
    xs[1:1 + Tt] = h3
    xx = (xs[0:Tt] - h3).reshape(R, D_MODEL)
    last = h3[Tt - 1]
    xs[0] = last
    shift_out[...] = last

    def mixed(m):
        return h + xx * mix_ref[m:m + 1, :]

    def out3(v):
        return v.reshape(Tt, Bb, D_MODEL)

    xv = mixed(3)
    lw = _bdot(mixed(1), w1_ref[...])
    write_out_some()
    la = _bdot(mixed(4), a1_ref[...])
    write_out_some()
    lg = _bdot(mixed(5), g1_ref[...])
    write_out_some()
    if has_vres:
        lv = _bdot(xv, v1_ref[...])
        write_out_some()
    nxt[0][...] = out3(_bdot(mixed(0), wr_ref[...]))
    write_out_some()
    nxt[2][...] = out3(_bdot(mixed(2), wk_ref[...]))
    write_out_some()
    v = _bdot(xv, wv_ref[...])
    write_out_some()
    wl = w0_ref[...] + _bdot(jnp.tanh(lw), w2_ref[...])
    w_log = -(jnp.maximum(-wl, 0.0) + jnp.log(1.0 + jnp.exp(-jnp.abs(wl)))) - 0.5
    nxt[1][...] = out3(jnp.exp(-jnp.exp(w_log)))
    write_out_some()
    nxt[4][...] = out3(_sigmoid(a0_ref[...] + _bdot(la, a2_ref[...])))
    write_out_some()
    nxt[PROJ_SEQS][...] = out3(_bdot(_sigmoid(lg), g2_ref[...]))
    write_out_some()
    if has_vres:
        vf = vfirst_ref[...].reshape(R, D_MODEL)
        v = v + (vf - v) * _sigmoid(v0_ref[...] + _bdot(lv, v2_ref[...]))
    nxt[V_IDX][...] = out3(v)
    assert not pieces

    for c, n in zip(cur, nxt):
        c[...] = n[...]


def _rwkv_proj(x, shift, v_first, W, i, j):
    T, B, D = x.shape
    Tt, Bb = _tiles(T, B)
    nT = T // Tt
    has_vres = v_first is not None
    row_in = pl.BlockSpec((Tt, Bb, D), lambda b, s: (jnp.minimum(s, nT - 1), b, 0))
    row_out = pl.BlockSpec((Tt, Bb, D), lambda b, s: (jnp.maximum(s - 1, 0), b, 0))
    st = pl.BlockSpec((Bb, D), lambda b, s: (b, 0))
    args = [x, shift]
    specs = [row_in, st]
    if has_vres:
        args.append(v_first)
        specs.append(row_in)
    names = ['rwkv_mix', 'rwkv_wr', 'rwkv_wk', 'rwkv_wv', 'rwkv_w0', 'rwkv_w1', 'rwkv_w2',
             'rwkv_a0', 'rwkv_a1', 'rwkv_a2', 'rwkv_g1', 'rwkv_g2']
    args.append(W['norm_gain'])
    specs.append(_lspec(W['norm_gain'], i))
    for n in names:
        args.append(W[n])
        specs.append(_lspec(W[n], j))
    if has_vres:
        for n in ('rwkv_v0', 'rwkv_v1', 'rwkv_v2'):
            args.append(W[n])
            specs.append(_lspec(W[n], j - 1))
    act = jax.ShapeDtypeStruct((T, B, D), _F32)
    n_lanes = Bb // SUBLANES * LANES
    lane_spec = pl.BlockSpec((Tt, HEAD_SIZE, n_lanes), lambda b, s: (jnp.maximum(s - 1, 0), 0, b))
    lane_act = jax.ShapeDtypeStruct((T, HEAD_SIZE, B // SUBLANES * LANES), _F32)
    n_nat = 1 if has_vres else 2
    outs = pl.pallas_call(
        functools.partial(_rwkv_proj_kernel, has_vres, Tt, Bb),
        grid=(B // Bb, nT + 1),
        in_specs=specs,
        out_specs=[lane_spec] * PROJ_SEQS + [row_out] * n_nat + [st],
        out_shape=[lane_act] * PROJ_SEQS + [act] * n_nat + [jax.ShapeDtypeStruct((B, D), _F32)],
        scratch_shapes=([pltpu.VMEM((1 + Tt, Bb, D), _F32)]
                        + [pltpu.VMEM((Tt, Bb, D), _F32)] * (2 * (PROJ_SEQS + 1))),
        compiler_params=_params(1),
        name='rwkv_proj_vres' if has_vres else 'rwkv_proj',
    )(*args)
    return outs


def _wkv_kernel(Tc, r_ref, w_ref, k_ref, v_ref, a_ref, kk_ref, ka_ref, rk_ref, lnw_ref,
                lnb_ref, s0_ref, o_ref, s_ref, an_s, bn_s, km_s):
    N = HEAD_SIZE

    @pl.when(pl.program_id(1) == 0)
    def _():
        s_ref[...] = s0_ref[...]

    k = k_ref[...]
    ag = a_ref[...]
    kk = k * kk_ref[...]
    kk = kk / jnp.maximum(jnp.sqrt(jnp.sum(kk * kk, axis=1, keepdims=True)), 1e-12)
    km = k * (1.0 + (ag - 1.0) * ka_ref[...])
    an_s[...] = -kk
    bn_s[...] = kk * ag
    km_s[...] = km

    def row(ref, t, j):
        return ref[t, pl.ds(j, 1), :]

    sa = jnp.zeros((N, LANES), _F32)
    for j in range(N):
        sa = sa + s_ref[j] * row(an_s, 0, j)

    def step(t, sa):
        tn = jnp.minimum(t + 1, Tc - 1)
        v = v_ref[t]
        y = jnp.zeros((N, LANES), _F32)
        sa_next = jnp.zeros((N, LANES), _F32)
        for j in range(N):
            sn = s_ref[j] * row(w_ref, t, j) + sa * row(bn_s, t, j) + v * row(km_s, t, j)
            s_ref[j] = sn
            y = y + sn * row(r_ref, t, j)
            sa_next = sa_next + sn * row(an_s, tn, j)
        o_ref[t] = y
        return sa_next

    lax.fori_loop(0, Tc, step, sa)

    y = o_ref[...]
    mu = jnp.mean(y, axis=1, keepdims=True)
    d = y - mu
    var = jnp.mean(d * d, axis=1, keepdims=True)
    yn = d * lax.rsqrt(var + LNX_EPS) * lnw_ref[...] + lnb_ref[...]
    bonus = jnp.sum(r_ref[...] * km * rk_ref[...], axis=1, keepdims=True) * v_ref[...]
    o_ref[...] = yn + bonus


def _lane_param(p):
    return jnp.repeat(p.reshape(N_HEADS, HEAD_SIZE).T, LANES // N_HEADS, axis=1)


def _wkv(r, w, k, v, a, s0, W, j):
    T, N, L = r.shape
    G = L // LANES
    B = G * SUBLANES
    assert N == HEAD_SIZE and N_HEADS * SUBLANES == LANES
    Tc = min(T, WKV_T_CHUNK)
    assert T % Tc == 0 and Tc % 2 == 0
    s0t = (s0.astype(_F32).reshape(G, SUBLANES, N_HEADS, N, N).transpose(4, 3, 0, 2, 1)
           .reshape(N, N, L))
    pars = [_lane_param(W[n][j].reshape(-1))
            for n in ('rwkv_k_k', 'rwkv_k_a', 'rwkv_r_k', 'rwkv_lnx_w', 'rwkv_lnx_b')]
    seq_spec = pl.BlockSpec((Tc, N, LANES), lambda g, t: (t, 0, g))
    st_spec = pl.BlockSpec((N, N, LANES), lambda g, t: (0, 0, g))
    par_spec = pl.BlockSpec((N, LANES), lambda g, t: (0, 0))
    o, sT = pl.pallas_call(
        functools.partial(_wkv_kernel, Tc),
        grid=(G, T // Tc),
        in_specs=[seq_spec] * 5 + [par_spec] * 5 + [st_spec],
        out_specs=[seq_spec, st_spec],
        out_shape=[jax.ShapeDtypeStruct((T, N, L), _F32),
                   jax.ShapeDtypeStruct((N, N, L), _F32)],
        scratch_shapes=[pltpu.VMEM((Tc, N, LANES), _F32)] * 3,
        compiler_params=_params(1),
        name='wkv_scan',
    )(r, w, k, v, a, *pars, s0t)
    s_new = (sT.reshape(N, N, G, N_HEADS, SUBLANES).transpose(2, 4, 3, 1, 0)
             .reshape(B, N_HEADS, N, N))
    return o, s_new


def _rwkv_out_kernel(Tt, Bb, x_ref, o_ref, g_ref, wo_ref, gain_ref, y_ref, on_s):
    R = Tt * Bb
    for gi in range(Bb // SUBLANES):
        _from_lanes(o_ref, on_s, Tt, gi * LANES, gi * SUBLANES)
    m = _bdot((on_s[...] * g_ref[...]).reshape(R, D_MODEL), wo_ref[...])
    y_ref[...] = x_ref[...] + _rms(m, gain_ref[1:2, :]).reshape(Tt, Bb, D_MODEL)


def _rwkv_out(x, o, g, W, i, j):
    T, B, D = x.shape
    Tt, Bb = _tiles(T, B)
    row = _row_spec(Tt, Bb)
    lane_spec = pl.BlockSpec((Tt, HEAD_SIZE, Bb // SUBLANES * LANES), lambda b, t: (t, 0, b))
    return pl.pallas_call(
        functools.partial(_rwkv_out_kernel, Tt, Bb),
        grid=(B // Bb, T // Tt),
        in_specs=[row, lane_spec, row, _lspec(W['rwkv_wo'], j), _lspec(W['norm_gain'], i)],
        out_specs=row,
        out_shape=jax.ShapeDtypeStruct((T, B, D), _F32),
        scratch_shapes=[pltpu.VMEM((Tt, Bb, D), _F32)],
        compiler_params=_params(1),
        name='rwkv_out',
    )(x, o, g, W['rwkv_wo'], W['norm_gain'])


def _conformer_kernel(Tt, Bb, x_ref, st_ref, gain_ref, win_ref, bin_ref, dww_ref, dwb_ref,
                      lnw_ref, lnb_ref, wout_ref, bout_ref, y_ref, st_out, buf, cbuf):
    R = Tt * Bb
    D = D_MODEL
    W1 = CONV_WIDTH - 1
    t = pl.program_id(1)
    x = x_ref[...].reshape(R, D)
    hb = _rms(x, gain_ref[0:1, :]).astype(_BF)

    @pl.when(t == 0)
    def _():
        buf[0:W1] = st_ref[...]

    def glu_cols(j):
        cs = slice(j * GLU_COLS, (j + 1) * GLU_COLS)
        gs = slice(D + j * GLU_COLS, D + (j + 1) * GLU_COLS)
        val = jnp.dot(hb, win_ref[:, cs], preferred_element_type=_F32) + bin_ref[:, cs]
        gate = jnp.dot(hb, win_ref[:, gs], preferred_element_type=_F32) + bin_ref[:, gs]
        buf[W1:W1 + Tt, :, cs] = (val * _sigmoid(gate)).reshape(Tt, Bb, GLU_COLS)

    n_groups = D // GLU_COLS
    glu_cols(0)
    for j in range(n_groups):
        if j + 1 < n_groups:
            glu_cols(j + 1)
        for c in range(j * GLU_COLS // LANES, (j + 1) * GLU_COLS // LANES):
            cs = slice(c * LANES, (c + 1) * LANES)
            acc = jnp.zeros((Tt, Bb, LANES), _F32) + dwb_ref[:, cs]
            for kk in range(CONV_WIDTH):
                acc = acc + buf[kk:kk + Tt, :, cs] * dww_ref[kk:kk + 1, cs]
            cbuf[:, :, cs] = acc
    new_hist = buf[Tt:Tt + W1]
    st_out[...] = new_hist
    buf[0:W1] = new_hist

    cv = cbuf[...].reshape(R, D)
    mu = jnp.mean(cv, axis=-1, keepdims=True)
    d = cv - mu
    var = jnp.mean(d * d, axis=-1, keepdims=True)
    ln = d * lax.rsqrt(var + LN_EPS) * lnw_ref[...] + lnb_ref[...]
    m = _bdot(ln * _sigmoid(ln), wout_ref[...]) + bout_ref[...]
    y_ref[...] = (x + _rms(m, gain_ref[1:2, :])).reshape(Tt, Bb, D)


def _conformer(x, st, W, i, j):
    T, B, D = x.shape
    Tt, Bb = _tiles(T, B)
    W1 = CONV_WIDTH - 1
    row = _row_spec(Tt, Bb)
    stspec = pl.BlockSpec((W1, Bb, D), lambda b, t: (0, b, 0))
    names = ['conv_w_in', 'conv_b_in', 'conv_dw_w', 'conv_dw_b', 'conv_ln_w', 'conv_ln_b',
             'conv_w_out', 'conv_b_out']
    y, st_new = pl.pallas_call(
        functools.partial(_conformer_kernel, Tt, Bb),
        grid=(B // Bb, T // Tt),
        in_specs=[row, stspec, _lspec(W['norm_gain'], i)] + [_lspec(W[n], j) for n in names],
        out_specs=[row, stspec],
        out_shape=[jax.ShapeDtypeStruct((T, B, D), _F32), jax.ShapeDtypeStruct((W1, B, D), _F32)],
        scratch_shapes=[pltpu.VMEM((W1 + Tt, Bb, D), _F32), pltpu.VMEM((Tt, Bb, D), _F32)],
        compiler_params=_params(1),
        name='conformer_conv',
    )(x, st, W['norm_gain'], *[W[n] for n in names])
    return y, st_new


def _conv_ffn_kernel(Tt, Bb, F, x_ref, st_ref, gain_ref, win_ref, dww_ref, dwb_ref, wout_ref,
                     y_ref, st_out, buf):
    R = Tt * Bb
    D = D_MODEL
    W1 = FFN_CONV_WIDTH - 1
    C = FFN_CHUNK
    t = pl.program_id(1)
    x = x_ref[...].reshape(R, D)
    hb = _rms(x, gain_ref[2:3, :]).astype(_BF)

    @pl.when(t == 0)
    def _():
        buf[0:W1] = st_ref[...]

    def up(c):
        for col0 in (c * C, F + c * C):
            cs = slice(col0, col0 + C)
            u = jnp.dot(hb, win_ref[:, cs], preferred_element_type=_F32)
            buf[W1:W1 + Tt, :, cs] = u.reshape(Tt, Bb, C)

    def conv(col0):
        cs = slice(col0, col0 + C)
        acc = dwb_ref[:, cs] + buf[W1:W1 + Tt, :, cs] * dww_ref[W1:W1 + 1, cs]
        for kk in range(W1):
            acc = acc + buf[kk:kk + Tt, :, cs] * dww_ref[kk:kk + 1, cs]
        return acc.reshape(R, C)

    n_chunks = F // C
    acc = jnp.zeros((R, D), _F32)
    up(0)
    for c in range(n_chunks):
        if c + 1 < n_chunks:
            up(c + 1)
        gc = conv(c * C)
        uc = conv(F + c * C)
        z = gc * _sigmoid(gc) * uc
        acc = acc + _bdot(z, wout_ref[c * C:(c + 1) * C, :])
    new_hist = buf[Tt:Tt + W1]
    st_out[...] = new_hist
    buf[0:W1] = new_hist
    y_ref[...] = (x + _rms(acc, gain_ref[3:4, :])).reshape(Tt, Bb, D)


def _conv_ffn(x, st, W, i):
    T, B, D = x.shape
    Tt, Bb = _tiles(T, B)
    F2 = W['ffn_w_in'].shape[-1]
    F = F2 // 2
    assert F % FFN_CHUNK == 0
    W1 = FFN_CONV_WIDTH - 1
    row = _row_spec(Tt, Bb)
    stspec = pl.BlockSpec((W1, Bb, F2), lambda b, t: (0, b, 0))
    names = ['ffn_w_in', 'ffn_dw_w', 'ffn_dw_b', 'ffn_w_out']
    y, st_new = pl.pallas_call(
        functools.partial(_conv_ffn_kernel, Tt, Bb, F),
        grid=(B // Bb, T // Tt),
        in_specs=[row, stspec, _lspec(W['norm_gain'], i)] + [_lspec(W[n], i) for n in names],
        out_specs=[row, stspec],
        out_shape=[jax.ShapeDtypeStruct((T, B, D), _F32), jax.ShapeDtypeStruct((W1, B, F2), _F32)],
        scratch_shapes=[pltpu.VMEM((W1 + Tt, Bb, F2), _F32)],
        compiler_params=_params(1),
        name='conv_ffn',
    )(x, st, W['norm_gain'], *[W[n] for n in names])
    return y, st_new


_MATRICES = ('rwkv_wr', 'rwkv_wk', 'rwkv_wv', 'rwkv_wo', 'rwkv_w1', 'rwkv_w2', 'rwkv_a1',
             'rwkv_a2', 'rwkv_v1', 'rwkv_v2', 'rwkv_g1', 'rwkv_g2', 'conv_w_in', 'conv_w_out',
             'ffn_w_in', 'ffn_w_out')
_VECTORS = ('rwkv_w0', 'rwkv_a0', 'rwkv_v0', 'conv_b_in', 'conv_dw_b', 'conv_ln_w', 'conv_ln_b',
            'conv_b_out', 'ffn_dw_b')


def _prep_weights(p):
    W = dict(p)
    for n in _MATRICES:
        W[n] = p[n].astype(_BF)
    for n in _VECTORS:
        W[n] = p[n].reshape(p[n].shape[0], 1, p[n].shape[1])
    return W


def _trunk(x, st_wkv, st_shift, st_glu, st_ffn, W):
    x = x.swapaxes(0, 1)
    v_first = None
    wkv_l, shift_l, glu_l, ffn_l = [], [], [], []
    for i in range(DEPTH):
        j = i // 2
        if i % 2 == 0:
            outs = _rwkv_proj(x, st_shift[j], v_first, W, i, j)
            r, w, k, v, a, gate = outs[:6]
            sh = outs[-1]
            if v_first is None:
                v_first = outs[6]
            o, s_new = _wkv(r, w, k, v, a, st_wkv[j], W, j)
            x = _rwkv_out(x, o, gate, W, i, j)
            wkv_l.append(s_new)
            shift_l.append(sh)
        else:
            x, gb = _conformer(x, st_glu[j].swapaxes(0, 1), W, i, j)
            glu_l.append(gb.swapaxes(0, 1))
        x, fb = _conv_ffn(x, st_ffn[i].swapaxes(0, 1), W, i)
        ffn_l.append(fb.swapaxes(0, 1))
    return (x.swapaxes(0, 1), jnp.stack(wkv_l), jnp.stack(shift_l), jnp.stack(glu_l),
            jnp.stack(ffn_l))


def kernel(x_prompt, x_sample, state_rwkv_wkv, state_rwkv_shift, state_conv_glu, state_ffn_conv, norm_gain, rwkv_mix, rwkv_wr, rwkv_wk, rwkv_wv, rwkv_wo, rwkv_w0, rwkv_w1, rwkv_w2, rwkv_a0, rwkv_a1, rwkv_a2, rwkv_v0, rwkv_v1, rwkv_v2, rwkv_g1, rwkv_g2, rwkv_k_k, rwkv_k_a, rwkv_r_k, rwkv_lnx_w, rwkv_lnx_b, conv_w_in, conv_b_in, conv_dw_w, conv_dw_b, conv_ln_w, conv_ln_b, conv_w_out, conv_b_out, ffn_w_in, ffn_dw_w, ffn_dw_b, ffn_w_out):
    p = dict(norm_gain=norm_gain, rwkv_mix=rwkv_mix, rwkv_wr=rwkv_wr, rwkv_wk=rwkv_wk,
             rwkv_wv=rwkv_wv, rwkv_wo=rwkv_wo, rwkv_w0=rwkv_w0, rwkv_w1=rwkv_w1,
             rwkv_w2=rwkv_w2, rwkv_a0=rwkv_a0, rwkv_a1=rwkv_a1, rwkv_a2=rwkv_a2,
             rwkv_v0=rwkv_v0, rwkv_v1=rwkv_v1, rwkv_v2=rwkv_v2, rwkv_g1=rwkv_g1,
             rwkv_g2=rwkv_g2, rwkv_k_k=rwkv_k_k, rwkv_k_a=rwkv_k_a, rwkv_r_k=rwkv_r_k,
             rwkv_lnx_w=rwkv_lnx_w, rwkv_lnx_b=rwkv_lnx_b, conv_w_in=conv_w_in,
             conv_b_in=conv_b_in, conv_dw_w=conv_dw_w, conv_dw_b=conv_dw_b,
             conv_ln_w=conv_ln_w, conv_ln_b=conv_ln_b, conv_w_out=conv_w_out,
             conv_b_out=conv_b_out, ffn_w_in=ffn_w_in, ffn_dw_w=ffn_dw_w,
             ffn_dw_b=ffn_dw_b, ffn_w_out=ffn_w_out)
    W = _prep_weights(p)
    bp = x_prompt.shape[0]
    n_rwkv, n_conv, depth = state_rwkv_wkv.shape[0], state_conv_glu.shape[0], state_ffn_conv.shape[0]
    z_wkv = jnp.zeros((n_rwkv, bp) + state_rwkv_wkv.shape[2:], _F32)
    z_shift = jnp.zeros((n_rwkv, bp, D_MODEL), _F32)
    z_glu = jnp.zeros((n_conv, bp) + state_conv_glu.shape[2:], _F32)
    z_ffn = jnp.zeros((depth, bp) + state_ffn_conv.shape[2:], _F32)
    yp, p_wkv, p_shift, p_glu, p_ffn = _trunk(x_prompt, z_wkv, z_shift, z_glu, z_ffn, W)
    ys, s_wkv, s_shift, s_glu, s_ffn = _trunk(x_sample, state_rwkv_wkv, state_rwkv_shift,
                                              state_conv_glu, state_ffn_conv, W)
    wdt = state_rwkv_wkv.dtype
    return (yp, ys, p_wkv.astype(wdt), p_shift, p_glu, p_ffn,
            s_wkv.astype(wdt), s_shift, s_glu, s_ffn)
```

```python
import functools

import jax
import jax.numpy as jnp
from jax import lax
from jax.experimental import pallas as pl
from jax.experimental.pallas import tpu as pltpu

D_MODEL = 1024
HEAD_SIZE = 64
N_HEADS = D_MODEL // HEAD_SIZE
DEPTH = 4
CONV_WIDTH = 31
FFN_CONV_WIDTH = 3
NORM_EPS = 1e-6
LN_EPS = 1e-5
LNX_EPS = 64e-5

LANES = 128
SUBLANES = 8
ROWS = 256
ROWS_LONG = 512
FFN_CHUNK = 256
GLU_COLS = 256
WKV_T_CHUNK = 32
VMEM_LIMIT = 56 * 1024 * 1024

_BF = jnp.bfloat16
_F32 = jnp.float32


def _tiles(T, B, rows=ROWS):
    Tt = min(T, rows // SUBLANES)
    Bb = min(B, rows // Tt)
    assert T % Tt == 0 and B % Bb == 0 and Bb % SUBLANES == 0, (T, B)
    return Tt, Bb


def _rows_long(T):
    return ROWS_LONG if T >= ROWS_LONG // SUBLANES else ROWS


def _lspec(arr, layer):
    nd = arr.ndim - 1
    return pl.BlockSpec((None,) + arr.shape[1:], lambda *_: (layer,) + (0,) * nd,
                        pipeline_mode=pl.Buffered(1))


def _rms(x, g):
    return x * lax.rsqrt(jnp.mean(x * x, axis=-1, keepdims=True) + NORM_EPS) * g


def _bdot(a, w):
    return jnp.dot(a.astype(_BF), w, preferred_element_type=_F32)


def _sigmoid(x):
    return 1.0 / (1.0 + jnp.exp(-x))


def _params(n_par):
    return pltpu.CompilerParams(
        dimension_semantics=("parallel",) * n_par + ("arbitrary",),
        vmem_limit_bytes=VMEM_LIMIT)


def _row_spec(Tt, Bb):
    return pl.BlockSpec((Tt, Bb, D_MODEL), lambda b, t: (t, b, 0))


def _head_halves(a, b, low):
    return (jnp.where(low, a, pltpu.roll(b, HEAD_SIZE, axis=1)),
            jnp.where(low, pltpu.roll(a, HEAD_SIZE, axis=1), b))


def _to_lanes(tile, dst_ref, Tc, lane0):
    for q in range(Tc // 2):
        _to_lanes_pair(tile, dst_ref, q, lane0)


def _to_lanes_pair(tile, dst_ref, q, lane0):
    low = lax.broadcasted_iota(jnp.int32, (SUBLANES, LANES), 1) < HEAD_SIZE
    ls = slice(lane0, lane0 + LANES)
    rows = []
    for p in range(D_MODEL // LANES):
        cs = slice(p * LANES, (p + 1) * LANES)
        rows.extend(_head_halves(tile(2 * q, cs), tile(2 * q + 1, cs), low))
    z = jnp.concatenate(rows, axis=0).T
    dst_ref[2 * q, :, ls] = z[:HEAD_SIZE]
    dst_ref[2 * q + 1, :, ls] = z[HEAD_SIZE:]


def _from_lanes(src_ref, dst_ref, Tc, lane0, seq0):
    low = lax.broadcasted_iota(jnp.int32, (SUBLANES, LANES), 1) < HEAD_SIZE
    ls = slice(lane0, lane0 + LANES)
    bs = slice(seq0, seq0 + SUBLANES)
    for q in range(Tc // 2):
        z = jnp.concatenate([src_ref[2 * q, :, ls], src_ref[2 * q + 1, :, ls]], axis=0).T
        for p in range(D_MODEL // LANES):
            cs = slice(p * LANES, (p + 1) * LANES)
            even = z[(2 * p) * SUBLANES:(2 * p + 1) * SUBLANES]
            odd = z[(2 * p + 1) * SUBLANES:(2 * p + 2) * SUBLANES]
            dst_ref[2 * q, bs, cs] = jnp.where(low, even, pltpu.roll(odd, HEAD_SIZE, axis=1))
            dst_ref[2 * q + 1, bs, cs] = jnp.where(low, pltpu.roll(even, HEAD_SIZE, axis=1), odd)


PROJ_SEQS = 5


def _rwkv_proj_kernel(has_vres, Tt, Bb, *refs):
    refs = list(refs)
    x_ref, shift_ref = refs[:2]
    refs = refs[2:]
    if has_vres:
        vfirst_ref = refs.pop(0)
    (gain_ref, mix_ref, wr_ref, wk_ref, wv_ref, w0_ref, w1_ref, w2_ref,
     a0_ref, a1_ref, a2_ref, g1_ref, g2_ref) = refs[:13]
    refs = refs[13:]
    if has_vres:
        v0_ref, v1_ref, v2_ref = refs[:3]
        refs = refs[3:]
    lane_outs = refs[:PROJ_SEQS]
    refs = refs[PROJ_SEQS:]
    g_out = refs.pop(0)
    if not has_vres:
        vn_out = refs.pop(0)
    shift_out, xs = refs[:2]
    cur = refs[2:2 + PROJ_SEQS + 1]
    nxt = refs[2 + PROJ_SEQS + 1:]
    V_IDX = 3

    R = Tt * Bb
    s = pl.program_id(1)

    @pl.when(s == 0)
    def _():
        xs[0] = shift_ref[...]
        for c in cur:
            c[...] = jnp.zeros((Tt, Bb, D_MODEL), _F32)

    g_out[...] = cur[PROJ_SEQS][...]
    if not has_vres:
        vn_out[...] = cur[V_IDX][...]
    pieces = [(c, out_ref, gi, q)
              for c, out_ref in zip(cur[:PROJ_SEQS], lane_outs)
              for gi in range(Bb // SUBLANES) for q in range(Tt // 2)]
    n_slots = 10 if has_vres else 9
    per_slot = -(-len(pieces) // n_slots)

    def write_out_some():
        for _ in range(min(per_slot, len(pieces))):
            c, out_ref, gi, q = pieces.pop(0)
            bs = slice(gi * SUBLANES, (gi + 1) * SUBLANES)
            _to_lanes_pair(lambda t, cs: c[t, bs, cs], out_ref, q, gi * LANES)

    h = _rms(x_ref[...].reshape(R, D_MODEL), gain_ref[0:1, :])
    h3 = h.reshape(Tt, Bb, D_MODEL)
    xs[1:1 + Tt] = h3
    xx = (xs[0:Tt] - h3).reshape(R, D_MODEL)
    last = h3[Tt - 1]
    xs[0] = last
    shift_out[...] = last

    def mixed(m):
        return h + xx * mix_ref[m:m + 1, :]

    def out3(v):
        return v.reshape(Tt, Bb, D_MODEL)

    xv = mixed(3)
    lw = _bdot(mixed(1), w1_ref[...])
    write_out_some()
    la = _bdot(mixed(4), a1_ref[...])
    write_out_some()
    lg = _bdot(mixed(5), g1_ref[...])
    write_out_some()
    if has_vres:
        lv = _bdot(xv, v1_ref[...])
        write_out_some()
    nxt[0][...] = out3(_bdot(mixed(0), wr_ref[...]))
    write_out_some()
    nxt[2][...] = out3(_bdot(mixed(2), wk_ref[...]))
    write_out_some()
    v = _bdot(xv, wv_ref[...])
    write_out_some()
    wl = w0_ref[...] + _bdot(jnp.tanh(lw), w2_ref[...])
    w_log = -(jnp.maximum(-wl, 0.0) + jnp.log(1.0 + jnp.exp(-jnp.abs(wl)))) - 0.5
    nxt[1][...] = out3(jnp.exp(-jnp.exp(w_log)))
    write_out_some()
    nxt[4][...] = out3(_sigmoid(a0_ref[...] + _bdot(la, a2_ref[...])))
    write_out_some()
    nxt[PROJ_SEQS][...] = out3(_bdot(_sigmoid(lg), g2_ref[...]))
    write_out_some()
    if has_vres:
        vf = vfirst_ref[...].reshape(R, D_MODEL)
        v = v + (vf - v) * _sigmoid(v0_ref[...] + _bdot(lv, v2_ref[...]))
    nxt[V_IDX][...] = out3(v)
    assert not pieces

    for c, n in zip(cur, nxt):
        c[...] = n[...]


def _rwkv_proj(x, shift, v_first, W, i, j):
    T, B, D = x.shape
    Tt, Bb = _tiles(T, B)
    nT = T // Tt
    has_vres = v_first is not None
    row_in = pl.BlockSpec((Tt, Bb, D), lambda b, s: (jnp.minimum(s, nT - 1), b, 0))
    row_out = pl.BlockSpec((Tt, Bb, D), lambda b, s: (jnp.maximum(s - 1, 0), b, 0))
    st = pl.BlockSpec((Bb, D), lambda b, s: (b, 0))
    args = [x, shift]
    specs = [row_in, st]
    if has_vres:
        args.append(v_first)
        specs.append(row_in)
    names = ['rwkv_mix', 'rwkv_wr', 'rwkv_wk', 'rwkv_wv', 'rwkv_w0', 'rwkv_w1', 'rwkv_w2',
             'rwkv_a0', 'rwkv_a1', 'rwkv_a2', 'rwkv_g1', 'rwkv_g2']
    args.append(W['norm_gain'])
    specs.append(_lspec(W['norm_gain'], i))
    for n in names:
        args.append(W[n])
        specs.append(_lspec(W[n], j))
    if has_vres:
        for n in ('rwkv_v0', 'rwkv_v1', 'rwkv_v2'):
            args.append(W[n])
            specs.append(_lspec(W[n], j - 1))
    act = jax.ShapeDtypeStruct((T, B, D), _F32)
    n_lanes = Bb // SUBLANES * LANES
    lane_spec = pl.BlockSpec((Tt, HEAD_SIZE, n_lanes), lambda b, s: (jnp.maximum(s - 1, 0), 0, b))
    lane_act = jax.ShapeDtypeStruct((T, HEAD_SIZE, B // SUBLANES * LANES), _F32)
    n_nat = 1 if has_vres else 2
    outs = pl.pallas_call(
        functools.partial(_rwkv_proj_kernel, has_vres, Tt, Bb),
        grid=(B // Bb, nT + 1),
        in_specs=specs,
        out_specs=[lane_spec] * PROJ_SEQS + [row_out] * n_nat + [st],
        out_shape=[lane_act] * PROJ_SEQS + [act] * n_nat + [jax.ShapeDtypeStruct((B, D), _F32)],
        scratch_shapes=([pltpu.VMEM((1 + Tt, Bb, D), _F32)]
                        + [pltpu.VMEM((Tt, Bb, D), _F32)] * (2 * (PROJ_SEQS + 1))),
        compiler_params=_params(1),
        name='rwkv_proj_vres' if has_vres else 'rwkv_proj',
    )(*args)
    return outs


def _wkv_kernel(Tc, r_ref, w_ref, k_ref, v_ref, a_ref, kk_ref, ka_ref, rk_ref, lnw_ref,
                lnb_ref, s0_ref, o_ref, s_ref, an_s, bn_s, km_s):
    N = HEAD_SIZE

    @pl.when(pl.program_id(1) == 0)
    def _():
        s_ref[...] = s0_ref[...]

    k = k_ref[...]
    ag = a_ref[...]
    kk = k * kk_ref[...]
    kk = kk / jnp.maximum(jnp.sqrt(jnp.sum(kk * kk, axis=1, keepdims=True)), 1e-12)
    km = k * (1.0 + (ag - 1.0) * ka_ref[...])
    an_s[...] = -kk
    bn_s[...] = kk * ag
    km_s[...] = km

    def row(ref, t, j):
        return ref[t, pl.ds(j, 1), :]

    sa = jnp.zeros((N, LANES), _F32)
    for j in range(N):
        sa = sa + s_ref[j] * row(an_s, 0, j)

    def step(t, sa):
        tn = jnp.minimum(t + 1, Tc - 1)
        v = v_ref[t]
        y = jnp.zeros((N, LANES), _F32)
        sa_next = jnp.zeros((N, LANES), _F32)
        for j in range(N):
            sn = s_ref[j] * row(w_ref, t, j) + sa * row(bn_s, t, j) + v * row(km_s, t, j)
            s_ref[j] = sn
            y = y + sn * row(r_ref, t, j)
            sa_next = sa_next + sn * row(an_s, tn, j)
        o_ref[t] = y
        return sa_next

    lax.fori_loop(0, Tc, step, sa)

    y = o_ref[...]
    mu = jnp.mean(y, axis=1, keepdims=True)
    d = y - mu
    var = jnp.mean(d * d, axis=1, keepdims=True)
    yn = d * lax.rsqrt(var + LNX_EPS) * lnw_ref[...] + lnb_ref[...]
    bonus = jnp.sum(r_ref[...] * km * rk_ref[...], axis=1, keepdims=True) * v_ref[...]
    o_ref[...] = yn + bonus


def _lane_param(p):
    return jnp.repeat(p.reshape(N_HEADS, HEAD_SIZE).T, LANES // N_HEADS, axis=1)


def _wkv(r, w, k, v, a, s0, W, j):
    T, N, L = r.shape
    G = L // LANES
    B = G * SUBLANES
    assert N == HEAD_SIZE and N_HEADS * SUBLANES == LANES
    Tc = min(T, WKV_T_CHUNK)
    assert T % Tc == 0 and Tc % 2 == 0
    s0t = (s0.astype(_F32).reshape(G, SUBLANES, N_HEADS, N, N).transpose(4, 3, 0, 2, 1)
           .reshape(N, N, L))
    pars = [_lane_param(W[n][j].reshape(-1))
            for n in ('rwkv_k_k', 'rwkv_k_a', 'rwkv_r_k', 'rwkv_lnx_w', 'rwkv_lnx_b')]
    seq_spec = pl.BlockSpec((Tc, N, LANES), lambda g, t: (t, 0, g))
    st_spec = pl.BlockSpec((N, N, LANES), lambda g, t: (0, 0, g))
    par_spec = pl.BlockSpec((N, LANES), lambda g, t: (0, 0))
    o, sT = pl.pallas_call(
        functools.partial(_wkv_kernel, Tc),
        grid=(G, T // Tc),
        in_specs=[seq_spec] * 5 + [par_spec] * 5 + [st_spec],
        out_specs=[seq_spec, st_spec],
        out_shape=[jax.ShapeDtypeStruct((T, N, L), _F32),
                   jax.ShapeDtypeStruct((N, N, L), _F32)],
        scratch_shapes=[pltpu.VMEM((Tc, N, LANES), _F32)] * 3,
        compiler_params=_params(1),
        name='wkv_scan',
    )(r, w, k, v, a, *pars, s0t)
    s_new = (sT.reshape(N, N, G, N_HEADS, SUBLANES).transpose(2, 4, 3, 1, 0)
             .reshape(B, N_HEADS, N, N))
    return o, s_new


def _rwkv_out_kernel(Tt, Bb, x_ref, o_ref, g_ref, wo_ref, gain_ref, y_ref, on_s):
    R = Tt * Bb
    for gi in range(Bb // SUBLANES):
        _from_lanes(o_ref, on_s, Tt, gi * LANES, gi * SUBLANES)
    m = _bdot((on_s[...] * g_ref[...]).reshape(R, D_MODEL), wo_ref[...])
    y_ref[...] = x_ref[...] + _rms(m, gain_ref[1:2, :]).reshape(Tt, Bb, D_MODEL)


def _rwkv_out(x, o, g, W, i, j):
    T, B, D = x.shape
    Tt, Bb = _tiles(T, B, _rows_long(T))
    row = _row_spec(Tt, Bb)
    lane_spec = pl.BlockSpec((Tt, HEAD_SIZE, Bb // SUBLANES * LANES), lambda b, t: (t, 0, b))
    return pl.pallas_call(
        functools.partial(_rwkv_out_kernel, Tt, Bb),
        grid=(B // Bb, T // Tt),
        in_specs=[row, lane_spec, row, _lspec(W['rwkv_wo'], j), _lspec(W['norm_gain'], i)],
        out_specs=row,
        out_shape=jax.ShapeDtypeStruct((T, B, D), _F32),
        scratch_shapes=[pltpu.VMEM((Tt, Bb, D), _F32)],
        compiler_params=_params(1),
        name='rwkv_out',
    )(x, o, g, W['rwkv_wo'], W['norm_gain'])


def _conformer_kernel(Tt, Bb, x_ref, st_ref, gain_ref, win_ref, bin_ref, dww_ref, dwb_ref,
                      lnw_ref, lnb_ref, wout_ref, bout_ref, y_ref, st_out, buf, cbuf):
    R = Tt * Bb
    D = D_MODEL
    W1 = CONV_WIDTH - 1
    t = pl.program_id(1)
    x = x_ref[...].reshape(R, D)
    hb = _rms(x, gain_ref[0:1, :]).astype(_BF)

    @pl.when(t == 0)
    def _():
        buf[0:W1] = st_ref[...]

    def glu_cols(j):
        cs = slice(j * GLU_COLS, (j + 1) * GLU_COLS)
        gs = slice(D + j * GLU_COLS, D + (j + 1) * GLU_COLS)
        val = jnp.dot(hb, win_ref[:, cs], preferred_element_type=_F32) + bin_ref[:, cs]
        gate = jnp.dot(hb, win_ref[:, gs], preferred_element_type=_F32) + bin_ref[:, gs]
        buf[W1:W1 + Tt, :, cs] = (val * _sigmoid(gate)).reshape(Tt, Bb, GLU_COLS)

    n_groups = D // GLU_COLS
    glu_cols(0)
    for j in range(n_groups):
        if j + 1 < n_groups:
            glu_cols(j + 1)
        for c in range(j * GLU_COLS // LANES, (j + 1) * GLU_COLS // LANES):
            cs = slice(c * LANES, (c + 1) * LANES)
            acc = jnp.zeros((Tt, Bb, LANES), _F32) + dwb_ref[:, cs]
            for kk in range(CONV_WIDTH):
                acc = acc + buf[kk:kk + Tt, :, cs] * dww_ref[kk:kk + 1, cs]
            cbuf[:, :, cs] = acc
    new_hist = buf[Tt:Tt + W1]
    st_out[...] = new_hist
    buf[0:W1] = new_hist

    cv = cbuf[...].reshape(R, D)
    mu = jnp.mean(cv, axis=-1, keepdims=True)
    d = cv - mu
    var = jnp.mean(d * d, axis=-1, keepdims=True)
    ln = d * lax.rsqrt(var + LN_EPS) * lnw_ref[...] + lnb_ref[...]
    m = _bdot(ln * _sigmoid(ln), wout_ref[...]) + bout_ref[...]
    y_ref[...] = (x + _rms(m, gain_ref[1:2, :])).reshape(Tt, Bb, D)


def _conformer(x, st, W, i, j):
    T, B, D = x.shape
    Tt, Bb = _tiles(T, B)
    W1 = CONV_WIDTH - 1
    row = _row_spec(Tt, Bb)
    stspec = pl.BlockSpec((W1, Bb, D), lambda b, t: (0, b, 0))
    names = ['conv_w_in', 'conv_b_in', 'conv_dw_w', 'conv_dw_b', 'conv_ln_w', 'conv_ln_b',
             'conv_w_out', 'conv_b_out']
    y, st_new = pl.pallas_call(
        functools.partial(_conformer_kernel, Tt, Bb),
        grid=(B // Bb, T // Tt),
        in_specs=[row, stspec, _lspec(W['norm_gain'], i)] + [_lspec(W[n], j) for n in names],
        out_specs=[row, stspec],
        out_shape=[jax.ShapeDtypeStruct((T, B, D), _F32), jax.ShapeDtypeStruct((W1, B, D), _F32)],
        scratch_shapes=[pltpu.VMEM((W1 + Tt, Bb, D), _F32), pltpu.VMEM((Tt, Bb, D), _F32)],
        compiler_params=_params(1),
        name='conformer_conv',
    )(x, st, W['norm_gain'], *[W[n] for n in names])
    return y, st_new


def _conv_ffn_kernel(Tt, Bb, F, x_ref, st_ref, gain_ref, win_ref, dww_ref, dwb_ref, wout_ref,
                     y_ref, st_out, buf):
    R = Tt * Bb
    D = D_MODEL
    W1 = FFN_CONV_WIDTH - 1
    C = FFN_CHUNK
    t = pl.program_id(1)
    x = x_ref[...].reshape(R, D)
    hb = _rms(x, gain_ref[2:3, :]).astype(_BF)

    @pl.when(t == 0)
    def _():
        buf[0:W1] = st_ref[...]

    def up(c):
        for col0 in (c * C, F + c * C):
            cs = slice(col0, col0 + C)
            u = jnp.dot(hb, win_ref[:, cs], preferred_element_type=_F32)
            buf[W1:W1 + Tt, :, cs] = u.reshape(Tt, Bb, C)

    def conv(col0):
        cs = slice(col0, col0 + C)
        acc = dwb_ref[:, cs] + buf[W1:W1 + Tt, :, cs] * dww_ref[W1:W1 + 1, cs]
        for kk in range(W1):
            acc = acc + buf[kk:kk + Tt, :, cs] * dww_ref[kk:kk + 1, cs]
        return acc.reshape(R, C)

    n_chunks = F // C
    acc = jnp.zeros((R, D), _F32)
    up(0)
    for c in range(n_chunks):
        if c + 1 < n_chunks:
            up(c + 1)
        gc = conv(c * C)
        uc = conv(F + c * C)
        z = gc * _sigmoid(gc) * uc
        acc = acc + _bdot(z, wout_ref[c * C:(c + 1) * C, :])
    new_hist = buf[Tt:Tt + W1]
    st_out[...] = new_hist
    buf[0:W1] = new_hist
    y_ref[...] = (x + _rms(acc, gain_ref[3:4, :])).reshape(Tt, Bb, D)


def _conv_ffn(x, st, W, i):
    T, B, D = x.shape
    Tt, Bb = _tiles(T, B, _rows_long(T))
    F2 = W['ffn_w_in'].shape[-1]
    F = F2 // 2
    assert F % FFN_CHUNK == 0
    W1 = FFN_CONV_WIDTH - 1
    row = _row_spec(Tt, Bb)
    stspec = pl.BlockSpec((W1, Bb, F2), lambda b, t: (0, b, 0))
    names = ['ffn_w_in', 'ffn_dw_w', 'ffn_dw_b', 'ffn_w_out']
    y, st_new = pl.pallas_call(
        functools.partial(_conv_ffn_kernel, Tt, Bb, F),
        grid=(B // Bb, T // Tt),
        in_specs=[row, stspec, _lspec(W['norm_gain'], i)] + [_lspec(W[n], i) for n in names],
        out_specs=[row, stspec],
        out_shape=[jax.ShapeDtypeStruct((T, B, D), _F32), jax.ShapeDtypeStruct((W1, B, F2), _F32)],
        scratch_shapes=[pltpu.VMEM((W1 + Tt, Bb, F2), _F32)],
        compiler_params=_params(1),
        name='conv_ffn',
    )(x, st, W['norm_gain'], *[W[n] for n in names])
    return y, st_new


_MATRICES = ('rwkv_wr', 'rwkv_wk', 'rwkv_wv', 'rwkv_wo', 'rwkv_w1', 'rwkv_w2', 'rwkv_a1',
             'rwkv_a2', 'rwkv_v1', 'rwkv_v2', 'rwkv_g1', 'rwkv_g2', 'conv_w_in', 'conv_w_out',
             'ffn_w_in', 'ffn_w_out')
_VECTORS = ('rwkv_w0', 'rwkv_a0', 'rwkv_v0', 'conv_b_in', 'conv_dw_b', 'conv_ln_w', 'conv_ln_b',
            'conv_b_out', 'ffn_dw_b')


def _prep_weights(p):
    W = dict(p)
    for n in _MATRICES:
        W[n] = p[n].astype(_BF)
    for n in _VECTORS:
        W[n] = p[n].reshape(p[n].shape[0], 1, p[n].shape[1])
    return W


def _trunk(x, st_wkv, st_shift, st_glu, st_ffn, W):
    x = x.swapaxes(0, 1)
    v_first = None
    wkv_l, shift_l, glu_l, ffn_l = [], [], [], []
    for i in range(DEPTH):
        j = i // 2
        if i % 2 == 0:
            outs = _rwkv_proj(x, st_shift[j], v_first, W, i, j)
            r, w, k, v, a, gate = outs[:6]
            sh = outs[-1]
            if v_first is None:
                v_first = outs[6]
            o, s_new = _wkv(r, w, k, v, a, st_wkv[j], W, j)
            x = _rwkv_out(x, o, gate, W, i, j)
            wkv_l.append(s_new)
            shift_l.append(sh)
        else:
            x, gb = _conformer(x, st_glu[j].swapaxes(0, 1), W, i, j)
            glu_l.append(gb.swapaxes(0, 1))
        x, fb = _conv_ffn(x, st_ffn[i].swapaxes(0, 1), W, i)
        ffn_l.append(fb.swapaxes(0, 1))
    return (x.swapaxes(0, 1), jnp.stack(wkv_l), jnp.stack(shift_l), jnp.stack(glu_l),
            jnp.stack(ffn_l))


def kernel(x_prompt, x_sample, state_rwkv_wkv, state_rwkv_shift, state_conv_glu, state_ffn_conv, norm_gain, rwkv_mix, rwkv_wr, rwkv_wk, rwkv_wv, rwkv_wo, rwkv_w0, rwkv_w1, rwkv_w2, rwkv_a0, rwkv_a1, rwkv_a2, rwkv_v0, rwkv_v1, rwkv_v2, rwkv_g1, rwkv_g2, rwkv_k_k, rwkv_k_a, rwkv_r_k, rwkv_lnx_w, rwkv_lnx_b, conv_w_in, conv_b_in, conv_dw_w, conv_dw_b, conv_ln_w, conv_ln_b, conv_w_out, conv_b_out, ffn_w_in, ffn_dw_w, ffn_dw_b, ffn_w_out):
    p = dict(norm_gain=norm_gain, rwkv_mix=rwkv_mix, rwkv_wr=rwkv_wr, rwkv_wk=rwkv_wk,
             rwkv_wv=rwkv_wv, rwkv_wo=rwkv_wo, rwkv_w0=rwkv_w0, rwkv_w1=rwkv_w1,
             rwkv_w2=rwkv_w2, rwkv_a0=rwkv_a0, rwkv_a1=rwkv_a1, rwkv_a2=rwkv_a2,
             rwkv_v0=rwkv_v0, rwkv_v1=rwkv_v1, rwkv_v2=rwkv_v2, rwkv_g1=rwkv_g1,
             rwkv_g2=rwkv_g2, rwkv_k_k=rwkv_k_k, rwkv_k_a=rwkv_k_a, rwkv_r_k=rwkv_r_k,
             rwkv_lnx_w=rwkv_lnx_w, rwkv_lnx_b=rwkv_lnx_b, conv_w_in=conv_w_in,
             conv_b_in=conv_b_in, conv_dw_w=conv_dw_w, conv_dw_b=conv_dw_b,
             conv_ln_w=conv_ln_w, conv_ln_b=conv_ln_b, conv_w_out=conv_w_out,
             conv_b_out=conv_b_out, ffn_w_in=ffn_w_in, ffn_dw_w=ffn_dw_w,
             ffn_dw_b=ffn_dw_b, ffn_w_out=ffn_w_out)
    W = _prep_weights(p)
    bp = x_prompt.shape[0]
    n_rwkv, n_conv, depth = state_rwkv_wkv.shape[0], state_conv_glu.shape[0], state_ffn_conv.shape[0]
    z_wkv = jnp.zeros((n_rwkv, bp) + state_rwkv_wkv.shape[2:], _F32)
    z_shift = jnp.zeros((n_rwkv, bp, D_MODEL), _F32)
    z_glu = jnp.zeros((n_conv, bp) + state_conv_glu.shape[2:], _F32)
    z_ffn = jnp.zeros((depth, bp) + state_ffn_conv.shape[2:], _F32)
    yp, p_wkv, p_shift, p_glu, p_ffn = _trunk(x_prompt, z_wkv, z_shift, z_glu, z_ffn, W)
    ys, s_wkv, s_shift, s_glu, s_ffn = _trunk(x_sample, state_rwkv_wkv, state_rwkv_shift,
                                              state_conv_glu, state_ffn_conv, W)
    wdt = state_rwkv_wkv.dtype
    return (yp, ys, p_wkv.astype(wdt), p_shift, p_glu, p_ffn,
            s_wkv.astype(wdt), s_shift, s_glu, s_ffn)
```
